```python
import math
import jax, jax.numpy as jnp
from jax import lax
import numpy as np

D_MODEL = 1024
BATCH = 16
SEQ = 256
DEPTH = 4
DEC_BATCH = 4
DEC_SEQ = 4096
PAST_LEN = 512

GRID_W = 64
N_MIXERS = 4
N_RET = (DEPTH + 3) // 4
N_CONV = (DEPTH + 2) // 4
N_POOL = (DEPTH + 1) // 4
N_FNET = DEPTH // 4
RET_HEADS = 4
RET_DK = D_MODEL // RET_HEADS
RET_DV = 2 * D_MODEL // RET_HEADS
RET_CHUNK = 128
RET_DECAY_BASE = 5.0
ROPE_BASE = 10000.0
CONV_WIDTH = 31
POOL_WINDOWS = (2, 4, 8, 16)
POOL_GROUPS = 4
POOL_GW = D_MODEL // POOL_GROUPS
FNET_GROUPS = 4
FNET_GW = D_MODEL // FNET_GROUPS
D_FF = 4 * D_MODEL
NORM_EPS = 1e-6
GN_EPS = 1e-5

kernel_name = 'hybrid_retention_conv_pool_fourier_flow_step'

F32 = jnp.float32


def rmsnorm(x, g):
    xf = x.astype(F32)
    y = xf * lax.rsqrt(jnp.mean(xf * xf, axis=-1, keepdims=True) + NORM_EPS)
    return (y * g.astype(F32)).astype(x.dtype)


def layernorm(x, g, b):
    xf = x.astype(F32)
    mu = jnp.mean(xf, axis=-1, keepdims=True)
    var = jnp.mean(jnp.square(xf - mu), axis=-1, keepdims=True)
    y = (xf - mu) * lax.rsqrt(var + GN_EPS)
    return (y * g.astype(F32) + b.astype(F32)).astype(x.dtype)


def adaln_params(cond, w, b):
    m = jax.nn.silu(cond) @ w + b
    return jnp.split(m[:, None, :], 6, axis=-1)


def modulate(x, shift, scale):
    return x * (1.0 + scale) + shift


def grid_rope_tables(T):
    rows = T // GRID_W
    row = jnp.repeat(jnp.arange(rows, dtype=F32), GRID_W)
    col = jnp.tile(jnp.arange(GRID_W, dtype=F32), rows)
    n = RET_DK // 4
    inv = ROPE_BASE ** (-jnp.arange(n, dtype=F32) / n)
    ang = jnp.concatenate([row[:, None] * inv, col[:, None] * inv], axis=-1)
    return jnp.cos(ang)[None, :, None, :], jnp.sin(ang)[None, :, None, :]


def apply_rope(x, cos, sin):
    x1, x2 = jnp.split(x, 2, axis=-1)
    return jnp.concatenate([x1 * cos - x2 * sin, x2 * cos + x1 * sin], axis=-1)


def log_decay(p):
    return jnp.log1p(-jnp.exp2(-p.astype(F32)))


def retention_scan(q, k, v, log_g, s0):
    B, T, H, _ = q.shape
    n = T // RET_CHUNK

    def chunks(a):
        return a.reshape(B, n, RET_CHUNK, H, a.shape[-1]).swapaxes(0, 1)

    idx = jnp.arange(RET_CHUNK, dtype=F32)
    rel = idx[:, None] - idx[None, :]
    d_intra = jnp.where(rel[None] >= 0, jnp.exp(log_g[:, None, None] * jnp.maximum(rel, 0.0)[None]), 0.0)
    xi = jnp.exp(log_g[None, :] * (idx[:, None] + 1.0))
    zeta = jnp.exp(log_g[None, :] * (RET_CHUNK - 1.0 - idx[:, None]))
    g_chunk = jnp.exp(log_g * RET_CHUNK)[None, :, None, None]

    def step(s, qkv):
        qi, ki, vi = qkv
        scores = jnp.einsum('bqhd,bkhd->bhqk', qi, ki) * d_intra[None]
        inner = jnp.einsum('bhqk,bkhe->bqhe', scores, vi)
        cross = jnp.einsum('bqhd,bhde->bqhe', qi, s) * xi[None, :, :, None]
        s = s * g_chunk + jnp.einsum('bkhd,bkhe->bhde', ki * zeta[None, :, :, None], vi)
        return s, inner + cross

    s_final, out = lax.scan(step, s0.astype(F32), (chunks(q), chunks(k), chunks(v)))
    return out.swapaxes(0, 1).reshape(B, T, H, v.shape[-1]), s_final


def head_groupnorm(o, g):
    B, T, H, E = o.shape
    mu = jnp.mean(o, axis=-1, keepdims=True)
    var = jnp.mean(jnp.square(o - mu), axis=-1, keepdims=True)
    y = ((o - mu) * lax.rsqrt(var + GN_EPS)).reshape(B, T, H * E)
    return y * g.astype(F32)


def retention_mixer(h, w_in, gn_g, w_out, decay_fwd, decay_bwd, s0_fwd, s0_bwd, rope):
    B, T, _ = h.shape
    hk = RET_HEADS * RET_DK
    hv = RET_HEADS * RET_DV
    q, k, v, g_f, g_b = jnp.split(h @ w_in, [hk, 2 * hk, 2 * hk + hv, 2 * hk + 2 * hv], axis=-1)
    q = q.astype(F32).reshape(B, T, RET_HEADS, RET_DK)
    k = k.astype(F32).reshape(B, T, RET_HEADS, RET_DK) * (RET_DK ** -0.5)
    v = v.astype(F32).reshape(B, T, RET_HEADS, RET_DV)
    if rope is not None:
        cos, sin = rope
        q = apply_rope(q, cos, sin)
        k = apply_rope(k, cos, sin)
    o_f, s_f = retention_scan(q, k, v, log_decay(decay_fwd), s0_fwd)
    o_b, s_b = retention_scan(q[:, ::-1], k[:, ::-1], v[:, ::-1], log_decay(decay_bwd), s0_bwd)
    o_b = o_b[:, ::-1]
    y = (jax.nn.silu(g_f) * head_groupnorm(o_f, gn_g).astype(h.dtype)
         + jax.nn.silu(g_b) * head_groupnorm(o_b, gn_g).astype(h.dtype))
    return y @ w_out, s_f, s_b


def conformer_conv(h, w_pw1, b_pw1, w_dw, b_dw, ln_g, ln_b, w_pw2, b_pw2):
    u = h @ w_pw1 + b_pw1
    a, gate = jnp.split(u, 2, axis=-1)
    u = a * jax.nn.sigmoid(gate)
    pad = CONV_WIDTH // 2
    u = lax.conv_general_dilated(u, w_dw[:, None, :].astype(u.dtype), (1,), [(pad, pad)],
                                 dimension_numbers=('NWC', 'WIO', 'NWC'),
                                 feature_group_count=D_MODEL) + b_dw
    u = jax.nn.silu(layernorm(u, ln_g, ln_b))
    return u @ w_pw2 + b_pw2


def pool_mix(h, w_grp, scale):
    B, T, _ = h.shape
    hf = h.astype(F32)
    cs = jnp.concatenate([jnp.zeros((B, 1, D_MODEL), F32), jnp.cumsum(hf, axis=1)], axis=1)
    t = jnp.arange(T)
    outs = []
    for gi, w in enumerate(POOL_WINDOWS):
        lo = jnp.clip(t - w // 2, 0, T)
        hi = jnp.clip(t + w // 2, 0, T)
        sl = slice(gi * POOL_GW, (gi + 1) * POOL_GW)
        s = cs[:, hi, sl] - cs[:, lo, sl]
        outs.append(s / (hi - lo).astype(F32)[None, :, None] - hf[:, :, sl])
    p = jnp.stack(outs, axis=2).astype(h.dtype)
    y = jnp.einsum('btgc,gcd->btgd', p, w_grp).reshape(B, T, D_MODEL)
    return y * scale


def fourier_mix(h, w, b):
    B, T, _ = h.shape
    hf = h.astype(F32).reshape(B, T, FNET_GROUPS, FNET_GW)
    f = jnp.fft.fftn(hf, axes=(1, 3), norm='ortho').real.reshape(B, T, D_MODEL)
    return f.astype(h.dtype) @ w + b


def sq_relu_mlp(h, w1, w2):
    return jnp.square(jax.nn.relu(h @ w1)) @ w2


def setup_inputs(seed: int = 0) -> dict:
    key = jax.random.key(seed)
    ks = iter(jax.random.split(key, 40))

    def nrm(shape, s):
        return jax.random.normal(next(ks), shape, F32) * s

    D = D_MODEL
    hk = RET_HEADS * RET_DK
    hv = RET_HEADS * RET_DV
    state_scale = (min(PAST_LEN, 256) ** 0.5) * (RET_DK ** -0.5)
    heads = jnp.arange(RET_HEADS, dtype=F32)
    return {
        'x_prompt': nrm((BATCH, SEQ, D), 1.0),
        'x_sample': nrm((DEC_BATCH, DEC_SEQ, D), 1.0),
        'state_ret_fwd': nrm((DEC_BATCH, N_RET, RET_HEADS, RET_DK, RET_DV), state_scale),
        'state_ret_bwd': nrm((DEC_BATCH, N_RET, RET_HEADS, RET_DK, RET_DV), state_scale),
        'c': nrm((DEC_BATCH, D), 1.0),
        'c_ctx': nrm((D,), 1.0),
        'w_mod': nrm((DEPTH, D, 6 * D), 0.5 * D ** -0.5),
        'b_mod': nrm((DEPTH, 6 * D), 0.01),
        'norm_mix_g': 1.0 + nrm((DEPTH, D), 0.02),
        'norm_mlp_g': 1.0 + nrm((DEPTH, D), 0.02),
        'mlp_w1': nrm((DEPTH, D, D_FF), D ** -0.5),
        'mlp_w2': nrm((DEPTH, D_FF, D), D_FF ** -0.5),
        'ret_w_in': nrm((N_RET, D, 2 * hk + 3 * hv), D ** -0.5),
        'ret_gn_g': 1.0 + nrm((N_RET, hv), 0.02),
        'ret_w_out': nrm((N_RET, hv, D), hv ** -0.5),
        'ret_decay_fwd': RET_DECAY_BASE + heads + nrm((N_RET, RET_HEADS), 0.1),
        'ret_decay_bwd': RET_DECAY_BASE + heads + nrm((N_RET, RET_HEADS), 0.1),
        'conv_w_pw1': nrm((N_CONV, D, 2 * D), D ** -0.5),
        'conv_b_pw1': nrm((N_CONV, 2 * D), 0.01),
        'conv_w_dw': nrm((N_CONV, CONV_WIDTH, D), CONV_WIDTH ** -0.5),
        'conv_b_dw': nrm((N_CONV, D), 0.01),
        'conv_ln_g': 1.0 + nrm((N_CONV, D), 0.02),
        'conv_ln_b': nrm((N_CONV, D), 0.01),
        'conv_w_pw2': nrm((N_CONV, D, D), D ** -0.5),
        'conv_b_pw2': nrm((N_CONV, D), 0.01),
        'pool_w': nrm((N_POOL, POOL_GROUPS, POOL_GW, POOL_GW), POOL_GW ** -0.5),
        'pool_scale': 1.0 + nrm((N_POOL, D), 0.1),
        'fnet_w': nrm((N_FNET, D, D), D ** -0.5),
        'fnet_b': nrm((N_FNET, D), 0.01),
        'final_norm_g': 1.0 + nrm((D,), 0.02),
    }


def reference(x_prompt, x_sample, state_ret_fwd, state_ret_bwd, c, c_ctx, w_mod, b_mod,
              norm_mix_g, norm_mlp_g, mlp_w1, mlp_w2, ret_w_in, ret_gn_g, ret_w_out,
              ret_decay_fwd, ret_decay_bwd, conv_w_pw1, conv_b_pw1, conv_w_dw, conv_b_dw,
              conv_ln_g, conv_ln_b, conv_w_pw2, conv_b_pw2, pool_w, pool_scale, fnet_w, fnet_b,
              final_norm_g):
    xp, xs = x_prompt, x_sample
    bp = xp.shape[0]
    rope = grid_rope_tables(xs.shape[1])
    zero_state = jnp.zeros((bp, RET_HEADS, RET_DK, RET_DV), F32)
    new_f, new_b = [], []
    for i in range(DEPTH):
        kind = i % N_MIXERS
        j = i // N_MIXERS
        mp = adaln_params(c_ctx[None, :], w_mod[i], b_mod[i])
        ms = adaln_params(c, w_mod[i], b_mod[i])
        hp = modulate(rmsnorm(xp, norm_mix_g[i]), mp[0], mp[1])
        hs = modulate(rmsnorm(xs, norm_mix_g[i]), ms[0], ms[1])
        if kind == 0:
            yp, sf, sb = retention_mixer(hp, ret_w_in[j], ret_gn_g[j], ret_w_out[j],
                                         ret_decay_fwd[j], ret_decay_bwd[j], zero_state, zero_state, None)
            ys, _, _ = retention_mixer(hs, ret_w_in[j], ret_gn_g[j], ret_w_out[j],
                                       ret_decay_fwd[j], ret_decay_bwd[j],
                                       state_ret_fwd[:, j], state_ret_bwd[:, j], rope)
            new_f.append(sf)
            new_b.append(sb)
        elif kind == 1:
            yp = conformer_conv(hp, conv_w_pw1[j], conv_b_pw1[j], conv_w_dw[j], conv_b_dw[j],
                                conv_ln_g[j], conv_ln_b[j], conv_w_pw2[j], conv_b_pw2[j])
            ys = conformer_conv(hs, conv_w_pw1[j], conv_b_pw1[j], conv_w_dw[j], conv_b_dw[j],
                                conv_ln_g[j], conv_ln_b[j], conv_w_pw2[j], conv_b_pw2[j])
        elif kind == 2:
            yp = pool_mix(hp, pool_w[j], pool_scale[j])
            ys = pool_mix(hs, pool_w[j], pool_scale[j])
        else:
            yp = fourier_mix(hp, fnet_w[j], fnet_b[j])
            ys = fourier_mix(hs, fnet_w[j], fnet_b[j])
        xp = xp + mp[2] * yp
        xs = xs + ms[2] * ys
        hp = modulate(rmsnorm(xp, norm_mlp_g[i]), mp[3], mp[4])
        hs = modulate(rmsnorm(xs, norm_mlp_g[i]), ms[3], ms[4])
        xp = xp + mp[5] * sq_relu_mlp(hp, mlp_w1[i], mlp_w2[i])
        xs = xs + ms[5] * sq_relu_mlp(hs, mlp_w1[i], mlp_w2[i])
    y_prompt = rmsnorm(xp, final_norm_g)
    y_sample = rmsnorm(xs, final_norm_g)
    new_state_ret_fwd = jnp.stack(new_f, axis=1)
    new_state_ret_bwd = jnp.stack(new_b, axis=1)
    return (y_prompt, y_sample, new_state_ret_fwd, new_state_ret_bwd)
```

```python
import functools
import math

import numpy as np
import jax
import jax.numpy as jnp
from jax import lax
from jax.experimental import pallas as pl
from jax.experimental.pallas import tpu as pltpu

F32 = jnp.float32
BF16 = jnp.bfloat16

D = 1024
D_FF = 4 * D
N_PROMPT_SEQ = 16
T_PROMPT = 256
N_SAMPLE_SEQ = 4
T_SAMPLE = 4096
GROUP_ROWS = 4096
N_GROUPS = 1 + N_SAMPLE_SEQ
N_ROWS = N_GROUPS * GROUP_ROWS
PROMPT_ROWS = N_PROMPT_SEQ * T_PROMPT
MOD_ROWS = 16
GRID_W = 64
HEADS = 4
DK = 256
DV = 512
HK = HEADS * DK
HV = HEADS * DV
RET_CHUNK = 128
ROPE_BASE = 10000.0
CONV_WIDTH = 31
CONV_PAD = CONV_WIDTH // 2
CONV_HALO = 16
POOL_WINDOWS = (2, 4, 8, 16)
POOL_HALO = 8
GW = 256
N_CGROUPS = D // GW
NORM_EPS = 1e-6
GN_EPS = 1e-5

TM = 1024
TILES_PER_GROUP = GROUP_ROWS // TM
TS = 256
VMEM_LIMIT = 56 * 1024 * 1024


def _cparams(sem):
    return pltpu.CompilerParams(dimension_semantics=sem, vmem_limit_bytes=VMEM_LIMIT)


def _norm_mod(x, g, shift, scale):
    ms = jnp.mean(x * x, axis=-1, keepdims=True)
    y = (x * lax.rsqrt(ms + NORM_EPS)) * g
    return y * (1.0 + scale) + shift


def _mod_slice(mod_ref, idx):
    return mod_ref[:, idx * D:(idx + 1) * D]


def _silu(x):
    return x * jax.nn.sigmoid(x)


def _mod_kernel(cond_ref, w_ref, b_ref, o_ref):
    a = _silu(cond_ref[...]).astype(BF16)
    o_ref[...] = jnp.dot(a, w_ref[...].astype(BF16), preferred_element_type=F32) + b_ref[...]


def _adaln_all(cond, w_mod, b_mod):
    depth = w_mod.shape[0]
    tn = 1536
    return pl.pallas_call(
        _mod_kernel,
        out_shape=jax.ShapeDtypeStruct((depth, MOD_ROWS, 6 * D), F32),
        grid=(depth, 6 * D // tn),
        in_specs=[
            pl.BlockSpec((MOD_ROWS, D), lambda l, j: (0, 0)),
            pl.BlockSpec((None, D, tn), lambda l, j: (l, 0, j)),
            pl.BlockSpec((None, 1, tn), lambda l, j: (l, 0, j)),
        ],
        out_specs=pl.BlockSpec((None, MOD_ROWS, tn), lambda l, j: (l, 0, j)),
        compiler_params=_cparams(("parallel", "parallel")),
        name="adaln_params",
    )(cond, w_mod, b_mod.reshape(depth, 1, 6 * D))


def _mod_spec():
    return pl.BlockSpec((None, 1, 6 * D), lambda i, j: (i // TILES_PER_GROUP, 0, 0))


def _nm_prologue(x_ref, mod_ref, g_ref, h_scr, shift_idx):
    @pl.when(pl.program_id(1) == 0)
    def _():
        h = _norm_mod(x_ref[...], g_ref[...], _mod_slice(mod_ref, shift_idx), _mod_slice(mod_ref, shift_idx + 1))
        h_scr[...] = h.astype(BF16)


def _nm_plain_kernel(x_ref, mod_ref, g_ref, w_ref, o_ref, h_scr, *, shift_idx):
    _nm_prologue(x_ref, mod_ref, g_ref, h_scr, shift_idx)
    o_ref[...] = jnp.dot(h_scr[...], w_ref[...], preferred_element_type=F32).astype(o_ref.dtype)


def _nm_rope_kernel(x_ref, mod_ref, g_ref, w_ref, cos_ref, sin_ref, o_ref, h_scr, *, shift_idx):
    _nm_prologue(x_ref, mod_ref, g_ref, h_scr, shift_idx)
    acc = jnp.dot(h_scr[...], w_ref[...], preferred_element_type=F32)
    half = DK // 2
    x1, x2 = acc[:, :half], acc[:, half:]
    cos, sin = cos_ref[...], sin_ref[...]
    rot = jnp.concatenate([x1 * cos - x2 * sin, x2 * cos + x1 * sin], axis=-1)
    kscale = jnp.where(pl.program_id(1) >= HEADS, DK ** -0.5, 1.0).astype(F32)
    o_ref[...] = (rot * kscale).astype(o_ref.dtype)


def _nm_glu_kernel(x_ref, mod_ref, g_ref, wa_ref, wg_ref, ba_ref, bg_ref, o_ref, h_scr, *, shift_idx):
    _nm_prologue(x_ref, mod_ref, g_ref, h_scr, shift_idx)
    h = h_scr[...]
    a = jnp.dot(h, wa_ref[...], preferred_element_type=F32) + ba_ref[...]
    gt = jnp.dot(h, wg_ref[...], preferred_element_type=F32) + bg_ref[...]
    o_ref[...] = (a * jax.nn.sigmoid(gt)).astype(o_ref.dtype)


def _nm_call(kernel, x, mod, g, extra_in, extra_specs, n_cols, tn, out_dtype, name):
    return pl.pallas_call(
        kernel,
        out_shape=jax.ShapeDtypeStruct((N_ROWS, n_cols), out_dtype),
        grid=(N_ROWS // TM, n_cols // tn),
        in_specs=[
            pl.BlockSpec((TM, D), lambda i, j: (i, 0)),
            _mod_spec(),
            pl.BlockSpec((1, D), lambda i, j: (0, 0)),
        ] + extra_specs,
        out_specs=pl.BlockSpec((TM, tn), lambda i, j: (i, j)),
        scratch_shapes=[pltpu.VMEM((TM, D), BF16)],
        compiler_params=_cparams(("parallel", "arbitrary")),
        name=name,
    )(x, mod, g, *extra_in)


def _nm_plain(x, mod, g, w, shift_idx, tn, out_dtype, name):
    n_cols = w.shape[1]
    return _nm_call(functools.partial(_nm_plain_kernel, shift_idx=shift_idx), x, mod, g,
                    [w], [pl.BlockSpec((D, tn), lambda i, j: (0, j))], n_cols, tn, out_dtype, name)


def _nm_rope(x, mod, g, w_qk, cos, sin, shift_idx):
    half = DK // 2
    return _nm_call(functools.partial(_nm_rope_kernel, shift_idx=shift_idx), x, mod, g,
                    [w_qk, cos, sin],
                    [pl.BlockSpec((D, DK), lambda i, j: (0, j)),
                     pl.BlockSpec((TM, half), lambda i, j: (i, 0)),
                     pl.BlockSpec((TM, half), lambda i, j: (i, 0))],
                    2 * HK, DK, BF16, "ret_qk_proj")


def _nm_glu(x, mod, g, w, b, shift_idx, tn):
    off = D // tn
    b2 = b.reshape(1, 2 * D)
    return _nm_call(functools.partial(_nm_glu_kernel, shift_idx=shift_idx), x, mod, g,
                    [w, w, b2, b2],
                    [pl.BlockSpec((D, tn), lambda i, j: (0, j)),
                     pl.BlockSpec((D, tn), lambda i, j: (0, j + off)),
                     pl.BlockSpec((1, tn), lambda i, j: (0, j)),
                     pl.BlockSpec((1, tn), lambda i, j: (0, j + off))],
                    D, tn, F32, "conv_pw1_glu")


def _out_kernel(yp_ref, ys_ref, w_ref, b_ref, x_ref, mod_ref, o_ref, *, gate_idx, n_prompt_tiles):
    i = pl.program_id(0)
    w = w_ref[...]

    def finish(y):
        acc = jnp.dot(y, w, preferred_element_type=F32) + b_ref[...]
        o_ref[...] = x_ref[...] + _mod_slice(mod_ref, gate_idx) * acc

    @pl.when(i < n_prompt_tiles)
    def _():
        finish(yp_ref[...])

    @pl.when(i >= n_prompt_tiles)
    def _():
        finish(ys_ref[...])


def _out_proj(y_prompt, y_sample, w, b, x, mod, gate_idx, name):
    k = w.shape[0]
    npt = PROMPT_ROWS // TM
    return pl.pallas_call(
        functools.partial(_out_kernel, gate_idx=gate_idx, n_prompt_tiles=npt),
        out_shape=jax.ShapeDtypeStruct((N_ROWS, D), F32),
        grid=(N_ROWS // TM,),
        in_specs=[
            pl.BlockSpec((TM, k), lambda i: (jnp.minimum(i, npt - 1), 0)),
            pl.BlockSpec((TM, k), lambda i: (jnp.maximum(i - npt, 0), 0)),
            pl.BlockSpec((k, D), lambda i: (0, 0)),
            pl.BlockSpec((1, D), lambda i: (0, 0)),
            pl.BlockSpec((TM, D), lambda i: (i, 0)),
            pl.BlockSpec((None, 1, 6 * D), lambda i: (i // TILES_PER_GROUP, 0, 0)),
        ],
        out_specs=pl.BlockSpec((TM, D), lambda i: (i, 0)),
        compiler_params=_cparams(("parallel",)),
        name=name,
    )(y_prompt, y_sample, w, b.reshape(1, D), x, mod)


def _mlp_kernel(x_ref, mod_ref, g_ref, w1_ref, w2_ref, fg_ref, o_ref, h_scr, acc_scr, *, n_ff, final_norm):
    f = pl.program_id(1)

    @pl.when(f == 0)
    def _():
        h = _norm_mod(x_ref[...], g_ref[...], _mod_slice(mod_ref, 3), _mod_slice(mod_ref, 4))
        h_scr[...] = h.astype(BF16)
        acc_scr[...] = jnp.zeros_like(acc_scr)

    hid = jnp.maximum(jnp.dot(h_scr[...], w1_ref[...], preferred_element_type=F32), 0.0)
    acc_scr[...] += jnp.dot((hid * hid).astype(BF16), w2_ref[...], preferred_element_type=F32)

    @pl.when(f == n_ff - 1)
    def _():
        y = x_ref[...] + _mod_slice(mod_ref, 5) * acc_scr[...]
        if final_norm:
            ms = jnp.mean(y * y, axis=-1, keepdims=True)
            y = (y * lax.rsqrt(ms + NORM_EPS)) * fg_ref[...]
        o_ref[...] = y


def _mlp(x, mod, g, w1, w2, final_g, row_tile0, n_tiles, final_norm, name):
    tf = 512
    n_ff = D_FF // tf
    return pl.pallas_call(
        functools.partial(_mlp_kernel, n_ff=n_ff, final_norm=final_norm),
        out_shape=jax.ShapeDtypeStruct((n_tiles * TM, D), F32),
        grid=(n_tiles, n_ff),
        in_specs=[
            pl.BlockSpec((TM, D), lambda i, f: (i + row_tile0, 0)),
            pl.BlockSpec((None, 1, 6 * D), lambda i, f: ((i + row_tile0) // TILES_PER_GROUP, 0, 0)),
            pl.BlockSpec((1, D), lambda i, f: (0, 0)),
            pl.BlockSpec((D, tf), lambda i, f: (0, f)),
            pl.BlockSpec((tf, D), lambda i, f: (f, 0)),
            pl.BlockSpec((1, D), lambda i, f: (0, 0)),
        ],
        out_specs=pl.BlockSpec((TM, D), lambda i, f: (i, 0)),
        scratch_shapes=[pltpu.VMEM((TM, D), BF16), pltpu.VMEM((TM, D), F32)],
        compiler_params=_cparams(("parallel", "arbitrary")),
        name=name,
    )(x, mod, g, w1, w2, final_g)


def _scan_kernel(*refs, chunk, n_chunks, reverse, has_s0, has_zin, emit_state):
    refs = list(refs)
    p_ref, q_ref, k_ref, v_ref, g_ref, gn_ref = refs[:6]
    pos = 6
    s0_ref = zin_ref = sout_ref = None
    if has_s0:
        s0_ref = refs[pos]; pos += 1
    if has_zin:
        zin_ref = refs[pos]; pos += 1
    z_ref = refs[pos]; pos += 1
    if emit_state:
        sout_ref = refs[pos]; pos += 1
    s_scr, d_scr, xi_scr, zeta_scr = refs[pos:pos + 4]

    c = pl.program_id(2)
    log_g = jnp.log1p(-jnp.exp2(-p_ref[...]))[:, :1]

    @pl.when(c == 0)
    def _():
        ri = lax.broadcasted_iota(jnp.int32, (chunk, chunk), 0)
        ci = lax.broadcasted_iota(jnp.int32, (chunk, chunk), 1)
        rel = (ci - ri) if reverse else (ri - ci)
        decay = jnp.exp(log_g * jnp.maximum(rel, 0).astype(F32))
        d_scr[...] = jnp.where(rel >= 0, decay, 0.0)
        t = lax.broadcasted_iota(jnp.int32, (chunk, 128), 0)
        step = ((chunk - 1) - t if reverse else t).astype(F32)
        xi_scr[...] = jnp.exp(log_g * (step + 1.0))
        zeta_scr[...] = jnp.exp(log_g * ((chunk - 1.0) - step))
        if has_s0:
            s_scr[...] = s0_ref[...]
        else:
            s_scr[...] = jnp.zeros_like(s_scr)

    q = q_ref[...]
    k = k_ref[...]
    v = v_ref[...]
    s = s_scr[...]
    scores = lax.dot_general(q, k, (((1,), (1,)), ((), ())), preferred_element_type=F32)
    inner = jnp.dot((scores * d_scr[...]).astype(BF16), v, preferred_element_type=F32)
    cross = jnp.dot(q, s.astype(BF16), preferred_element_type=F32)
    xi = jnp.concatenate([xi_scr[...]] * (DV // 128), axis=-1)
    o = inner + cross * xi
    zeta = jnp.concatenate([zeta_scr[...]] * (DK // 128), axis=-1)
    kz = (k.astype(F32) * zeta).astype(BF16)
    upd = lax.dot_general(kz, v, (((0,), (0,)), ((), ())), preferred_element_type=F32)
    s_new = s * jnp.exp(log_g * float(chunk)) + upd
    s_scr[...] = s_new

    mu = jnp.mean(o, axis=-1, keepdims=True)
    dlt = o - mu
    var = jnp.mean(dlt * dlt, axis=-1, keepdims=True)
    z = _silu(g_ref[...].astype(F32)) * ((dlt * lax.rsqrt(var + GN_EPS)) * gn_ref[...])
    if has_zin:
        z = zin_ref[...] + z
    z_ref[...] = z.astype(z_ref.dtype)

    if emit_state:
        @pl.when(c == n_chunks - 1)
        def _():
            sout_ref[...] = s_new


def _scan(decay_p, qk, vg, gn_g, s0, zin, *, row0, n_seq, seq_len, reverse, emit_state, out_dtype, name):
    chunk = RET_CHUNK
    nc = seq_len // chunk
    rb0 = row0 // chunk

    def rb(b, c):
        cc = (nc - 1 - c) if reverse else c
        return b * nc + cc

    gate_blk = HEADS + (HEADS if reverse else 0)
    in_specs = [
        pl.BlockSpec((None, 1, 128), lambda b, h, c: (h, 0, 0)),
        pl.BlockSpec((chunk, DK), lambda b, h, c: (rb0 + rb(b, c), h)),
        pl.BlockSpec((chunk, DK), lambda b, h, c: (rb0 + rb(b, c), HEADS + h)),
        pl.BlockSpec((chunk, DV), lambda b, h, c: (rb0 + rb(b, c), h)),
        pl.BlockSpec((chunk, DV), lambda b, h, c: (rb0 + rb(b, c), gate_blk + h)),
        pl.BlockSpec((1, DV), lambda b, h, c: (0, h)),
    ]
    args = [decay_p, qk, qk, vg, vg, gn_g]
    if s0 is not None:
        in_specs.append(pl.BlockSpec((None, None, DK, DV), lambda b, h, c: (b, h, 0, 0)))
        args.append(s0)
    if zin is not None:
        in_specs.append(pl.BlockSpec((chunk, DV), lambda b, h, c: (rb(b, c), h)))
        args.append(zin)
    out_shape = [jax.ShapeDtypeStruct((n_seq * seq_len, HV), out_dtype)]
    out_specs = [pl.BlockSpec((chunk, DV), lambda b, h, c: (rb(b, c), h))]
    if emit_state:
        out_shape.append(jax.ShapeDtypeStruct((n_seq, HEADS, DK, DV), F32))
        out_specs.append(pl.BlockSpec((None, None, DK, DV), lambda b, h, c: (b, h, 0, 0)))
    res = pl.pallas_call(
        functools.partial(_scan_kernel, chunk=chunk, n_chunks=nc, reverse=reverse,
                          has_s0=s0 is not None, has_zin=zin is not None, emit_state=emit_state),
        out_shape=out_shape,
        grid=(n_seq, HEADS, nc),
        in_specs=in_specs,
        out_specs=out_specs,
        scratch_shapes=[pltpu.VMEM((DK, DV), F32), pltpu.VMEM((chunk, chunk), F32),
                        pltpu.VMEM((chunk, 128), F32), pltpu.VMEM((chunk, 128), F32)],
        compiler_params=_cparams(("parallel", "parallel", "arbitrary")),
        name=name,
    )(*args)
    return res if emit_state else (res[0], None)


def _rope_tables():
    n = DK // 4
    inv = ROPE_BASE ** (-jnp.arange(n, dtype=F32) / n)
    t = jnp.arange(T_SAMPLE, dtype=jnp.int32)
    row = (t // GRID_W).astype(F32)
    col = (t % GRID_W).astype(F32)
    ang = jnp.concatenate([row[:, None] * inv[None, :], col[:, None] * inv[None, :]], axis=-1)
    cos = jnp.concatenate([jnp.ones((PROMPT_ROWS, DK // 2), F32)] + [jnp.cos(ang)] * N_SAMPLE_SEQ, axis=0)
    sin = jnp.concatenate([jnp.zeros((PROMPT_ROWS, DK // 2), F32)] + [jnp.sin(ang)] * N_SAMPLE_SEQ, axis=0)
    return cos, sin


def _retention_layer(x, mod, g, w_in, gn_g, w_out, decay_fwd, decay_bwd, s0_fwd, s0_bwd):
    w_in = w_in.astype(BF16)
    cos, sin = _rope_tables()
    qk = _nm_rope(x, mod, g, w_in[:, :2 * HK], cos, sin, 0)
    vg = _nm_plain(x, mod, g, w_in[:, 2 * HK:], 0, 512, BF16, "ret_vg_proj")
    gn = gn_g.reshape(1, HV)
    pf = jnp.broadcast_to(decay_fwd.astype(F32)[:, None, None], (HEADS, 1, 128))
    pb = jnp.broadcast_to(decay_bwd.astype(F32)[:, None, None], (HEADS, 1, 128))
    prompt = dict(row0=0, n_seq=N_PROMPT_SEQ, seq_len=T_PROMPT, emit_state=True)
    sample = dict(row0=PROMPT_ROWS, n_seq=N_SAMPLE_SEQ, seq_len=T_SAMPLE, emit_state=False)
    zp, sf = _scan(pf, qk, vg, gn, None, None, reverse=False, out_dtype=F32, name="ret_scan_prompt_fwd", **prompt)
    yp, sb = _scan(pb, qk, vg, gn, None, zp, reverse=True, out_dtype=BF16, name="ret_scan_prompt_bwd", **prompt)
    zs, _ = _scan(pf, qk, vg, gn, s0_fwd, None, reverse=False, out_dtype=F32, name="ret_scan_sample_fwd", **sample)
    ys, _ = _scan(pb, qk, vg, gn, s0_bwd, zs, reverse=True, out_dtype=BF16, name="ret_scan_sample_bwd", **sample)
    x = _out_proj(yp, ys, w_out.astype(BF16), jnp.zeros((D,), F32), x, mod, 2, "ret_out_proj")
    return x, sf, sb


def _seq_tile_flags(i):
    n_prompt_tiles = PROMPT_ROWS // TS
    tiles_per_seq = T_SAMPLE // TS
    is_sample = i >= n_prompt_tiles
    tin = (i - n_prompt_tiles) % tiles_per_seq
    if T_PROMPT != TS:
        raise NotImplementedError("prompt sequences must be exactly one row tile")
    return is_sample & (tin > 0), is_sample & (tin < tiles_per_seq - 1)


def _conv_kernel(u_ref, up_ref, un_ref, x_ref, mod_ref, wdw_ref, bdw_ref, lng_ref, lnb_ref, w2_ref, b2_ref,
                 o_ref, ext_scr, cv_scr):
    has_prev, has_next = _seq_tile_flags(pl.program_id(0))
    n_slabs = D // 128
    for lt in range(n_slabs):
        lanes = slice(lt * 128, (lt + 1) * 128)
        ext_scr[lt, 0:CONV_HALO, :] = jnp.where(has_prev, up_ref[:, lanes], 0.0)
        ext_scr[lt, CONV_HALO:CONV_HALO + TS, :] = u_ref[:, lanes]
        ext_scr[lt, CONV_HALO + TS:, :] = jnp.where(has_next, un_ref[:, lanes], 0.0)

    rows = 32
    shift = CONV_HALO - CONV_PAD

    def body(lt, carry):
        for r0 in range(0, TS, rows):
            acc = jnp.broadcast_to(bdw_ref[lt], (rows, 128))
            for kk in range(CONV_WIDTH):
                acc = acc + wdw_ref[lt, kk:kk + 1, :] * ext_scr[lt, r0 + kk + shift:r0 + kk + shift + rows, :]
            cv_scr[lt, r0:r0 + rows, :] = acc
        return carry

    lax.fori_loop(0, n_slabs, body, 0)

    cv = jnp.concatenate([cv_scr[lt] for lt in range(n_slabs)], axis=-1)
    mu = jnp.mean(cv, axis=-1, keepdims=True)
    dlt = cv - mu
    var = jnp.mean(dlt * dlt, axis=-1, keepdims=True)
    ln = (dlt * lax.rsqrt(var + GN_EPS)) * lng_ref[...] + lnb_ref[...]
    act = _silu(ln).astype(BF16)
    y = jnp.dot(act, w2_ref[...], preferred_element_type=F32) + b2_ref[...]
    o_ref[...] = x_ref[...] + _mod_slice(mod_ref, 2) * y


def _conv_layer(x, mod, g, w_pw1, b_pw1, w_dw, b_dw, ln_g, ln_b, w_pw2, b_pw2):
    u = _nm_glu(x, mod, g, w_pw1.astype(BF16), b_pw1, 0, 512)
    hb = TS // CONV_HALO
    last = N_ROWS // CONV_HALO - 1
    row = lambda a: a.reshape(1, D)
    return pl.pallas_call(
        _conv_kernel,
        out_shape=jax.ShapeDtypeStruct((N_ROWS, D), F32),
        grid=(N_ROWS // TS,),
        in_specs=[
            pl.BlockSpec((TS, D), lambda i: (i, 0)),
            pl.BlockSpec((CONV_HALO, D), lambda i: (jnp.maximum(i * hb - 1, 0), 0)),
            pl.BlockSpec((CONV_HALO, D), lambda i: (jnp.minimum((i + 1) * hb, last), 0)),
            pl.BlockSpec((TS, D), lambda i: (i, 0)),
            pl.BlockSpec((None, 1, 6 * D), lambda i: (i // (GROUP_ROWS // TS), 0, 0)),
            pl.BlockSpec((D // 128, CONV_WIDTH, 128), lambda i: (0, 0, 0)),
            pl.BlockSpec((D // 128, 1, 128), lambda i: (0, 0, 0)),
            pl.BlockSpec((1, D), lambda i: (0, 0)),
            pl.BlockSpec((1, D), lambda i: (0, 0)),
            pl.BlockSpec((D, D), lambda i: (0, 0)),
            pl.BlockSpec((1, D), lambda i: (0, 0)),
        ],
        out_specs=pl.BlockSpec((TS, D), lambda i: (i, 0)),
        scratch_shapes=[pltpu.VMEM((D // 128, TS + 2 * CONV_HALO, 128), F32), pltpu.VMEM((D // 128, TS, 128), F32)],
        compiler_params=_cparams(("parallel",)),
        name="conv_dw_ln_pw2",
    )(u, u, u, x, mod, w_dw.reshape(CONV_WIDTH, D // 128, 128).transpose(1, 0, 2), b_dw.reshape(D // 128, 1, 128),
      row(ln_g), row(ln_b), w_pw2.astype(BF16), row(b_pw2))


def _pool_kernel(x_ref, xp_ref, xn_ref, mod_ref, g_ref, w_ref, sc_ref, o_ref, ext_scr):
    i = pl.program_id(0)
    has_prev, has_next = _seq_tile_flags(i)
    shift, scale = _mod_slice(mod_ref, 0), _mod_slice(mod_ref, 1)
    g = g_ref[...]
    x = x_ref[...]
    h = _norm_mod(x, g, shift, scale)
    ext_scr[0:POOL_HALO, :] = jnp.where(has_prev, _norm_mod(xp_ref[...], g, shift, scale), 0.0)
    ext_scr[POOL_HALO:POOL_HALO + TS, :] = h
    ext_scr[POOL_HALO + TS:, :] = jnp.where(has_next, _norm_mod(xn_ref[...], g, shift, scale), 0.0)

    seq_len = jnp.where(i >= PROMPT_ROWS // TS, T_SAMPLE, T_PROMPT)
    t = (i * TS) % seq_len + lax.broadcasted_iota(jnp.int32, (TS, GW), 0)
    outs = []
    for gi, w in enumerate(POOL_WINDOWS):
        lanes = slice(gi * GW, (gi + 1) * GW)
        tot = ext_scr[pl.ds(POOL_HALO - w // 2, TS), lanes]
        for j in range(1, w):
            tot = tot + ext_scr[pl.ds(POOL_HALO - w // 2 + j, TS), lanes]
        cnt = jnp.minimum(t + w // 2, seq_len) - jnp.maximum(t - w // 2, 0)
        p = tot / cnt.astype(F32) - h[:, lanes]
        outs.append(jnp.dot(p.astype(BF16), w_ref[gi], preferred_element_type=F32))
    y = jnp.concatenate(outs, axis=-1) * sc_ref[...]
    o_ref[...] = x + _mod_slice(mod_ref, 2) * y


def _pool_layer(x, mod, g, w_grp, scale):
    hb = TS // POOL_HALO
    last = N_ROWS // POOL_HALO - 1
    return pl.pallas_call(
        _pool_kernel,
        out_shape=jax.ShapeDtypeStruct((N_ROWS, D), F32),
        grid=(N_ROWS // TS,),
        in_specs=[
            pl.BlockSpec((TS, D), lambda i: (i, 0)),
            pl.BlockSpec((POOL_HALO, D), lambda i: (jnp.maximum(i * hb - 1, 0), 0)),
            pl.BlockSpec((POOL_HALO, D), lambda i: (jnp.minimum((i + 1) * hb, last), 0)),
            pl.BlockSpec((None, 1, 6 * D), lambda i: (i // (GROUP_ROWS // TS), 0, 0)),
            pl.BlockSpec((1, D), lambda i: (0, 0)),
            pl.BlockSpec((N_CGROUPS, GW, GW), lambda i: (0, 0, 0)),
            pl.BlockSpec((1, D), lambda i: (0, 0)),
        ],
        out_specs=pl.BlockSpec((TS, D), lambda i: (i, 0)),
        scratch_shapes=[pltpu.VMEM((TS + 2 * POOL_HALO, D), F32)],
        compiler_params=_cparams(("parallel",)),
        name="pool_mixer",
    )(x, x, x, mod, g, w_grp.astype(BF16), scale.reshape(1, D))


def _chan_dft_kernel(x_ref, mod_ref, g_ref, c_ref, s_ref, a_ref, b_ref):
    h = _norm_mod(x_ref[...], g_ref[...], _mod_slice(mod_ref, 0), _mod_slice(mod_ref, 1)).astype(BF16)
    c, s = c_ref[...], s_ref[...]
    for gi in range(N_CGROUPS):
        lanes = slice(gi * GW, (gi + 1) * GW)
        a_ref[:, lanes] = jnp.dot(h[:, lanes], c, preferred_element_type=F32).astype(a_ref.dtype)
        b_ref[:, lanes] = jnp.dot(h[:, lanes], s, preferred_element_type=F32).astype(b_ref.dtype)


def _dft_tables(n):
    idx = jnp.arange(n, dtype=jnp.int32)
    ang = (2.0 * np.pi / n) * ((idx[:, None] * idx[None, :]) % n).astype(F32)
    return jnp.cos(ang), jnp.sin(ang)


def _chan_dft(x, mod, g):
    c, s = _dft_tables(GW)
    return pl.pallas_call(
        _chan_dft_kernel,
        out_shape=[jax.ShapeDtypeStruct((N_ROWS, D), BF16)] * 2,
        grid=(N_ROWS // TM,),
        in_specs=[
            pl.BlockSpec((TM, D), lambda i: (i, 0)),
            pl.BlockSpec((None, 1, 6 * D), lambda i: (i // TILES_PER_GROUP, 0, 0)),
            pl.BlockSpec((1, D), lambda i: (0, 0)),
            pl.BlockSpec((GW, GW), lambda i: (0, 0)),
            pl.BlockSpec((GW, GW), lambda i: (0, 0)),
        ],
        out_specs=[pl.BlockSpec((TM, D), lambda i: (i, 0))] * 2,
        compiler_params=_cparams(("parallel",)),
        name="fnet_chan_dft",
    )(x, mod, g, c.astype(BF16), s.astype(BF16))


def _time_dft_kernel(c_ref, sn_ref, a_ref, b_ref, o_ref, acc_scr, *, n_k, scale):
    k = pl.program_id(2)

    @pl.when(k == 0)
    def _():
        acc_scr[...] = jnp.zeros_like(acc_scr)

    acc_scr[...] += (jnp.dot(c_ref[...], a_ref[...], preferred_element_type=F32)
                     + jnp.dot(sn_ref[...], b_ref[...], preferred_element_type=F32))

    @pl.when(k == n_k - 1)
    def _():
        o_ref[...] = (acc_scr[...] * scale).astype(o_ref.dtype)


def _time_dft(cos_t, nsin_t, a, b, *, row0, n_seq, seq_len, name):
    tm = min(seq_len, 1024)
    tk = min(seq_len, 512)
    n_i, n_k = seq_len // tm, seq_len // tk
    kb0 = row0 // tk
    scale = 1.0 / math.sqrt(seq_len * GW)
    return pl.pallas_call(
        functools.partial(_time_dft_kernel, n_k=n_k, scale=scale),
        out_shape=jax.ShapeDtypeStruct((n_seq * seq_len, D), BF16),
        grid=(n_seq, n_i, n_k),
        in_specs=[
            pl.BlockSpec((tm, tk), lambda s, i, k: (i, k)),
            pl.BlockSpec((tm, tk), lambda s, i, k: (i, k)),
            pl.BlockSpec((tk, D), lambda s, i, k: (kb0 + s * n_k + k, 0)),
            pl.BlockSpec((tk, D), lambda s, i, k: (kb0 + s * n_k + k, 0)),
        ],
        out_specs=pl.BlockSpec((tm, D), lambda s, i, k: (s * n_i + i, 0)),
        scratch_shapes=[pltpu.VMEM((tm, D), F32)],
        compiler_params=_cparams(("parallel", "parallel", "arbitrary")),
        name=name,
    )(cos_t, nsin_t, a, b)


def _big_dft_tables(n, n1):
    n2 = n // n1
    f = jnp.arange(n, dtype=jnp.int32)[:, None]
    ang1 = (2.0 * np.pi / n1) * ((f * jnp.arange(n1, dtype=jnp.int32)[None, :]) % n1).astype(F32)
    ang2 = (2.0 * np.pi / n) * ((f * jnp.arange(n2, dtype=jnp.int32)[None, :]) % n).astype(F32)
    c1, s1 = jnp.cos(ang1)[:, :, None], jnp.sin(ang1)[:, :, None]
    c2, s2 = jnp.cos(ang2)[:, None, :], jnp.sin(ang2)[:, None, :]
    cos_t = (c1 * c2 - s1 * s2).reshape(n, n)
    sin_t = (s1 * c2 + c1 * s2).reshape(n, n)
    return cos_t.astype(BF16), (-sin_t).astype(BF16)


def _fourier_layer(x, mod, g, w, b):
    a, bb = _chan_dft(x, mod, g)
    cp, sp = _dft_tables(T_PROMPT)
    fp = _time_dft(cp.astype(BF16), (-sp).astype(BF16), a, bb,
                   row0=0, n_seq=N_PROMPT_SEQ, seq_len=T_PROMPT, name="fnet_time_dft_prompt")
    cs, ss = _big_dft_tables(T_SAMPLE, GRID_W)
    fs = _time_dft(cs, ss, a, bb, row0=PROMPT_ROWS, n_seq=N_SAMPLE_SEQ, seq_len=T_SAMPLE,
                   name="fnet_time_dft_sample")
    return _out_proj(fp, fs, w.astype(BF16), b, x, mod, 2, "fnet_out_proj")


def kernel(x_prompt, x_sample, state_ret_fwd, state_ret_bwd, c, c_ctx, w_mod, b_mod, norm_mix_g, norm_mlp_g, mlp_w1, mlp_w2, ret_w_in, ret_gn_g, ret_w_out, ret_decay_fwd, ret_decay_bwd, conv_w_pw1, conv_b_pw1, conv_w_dw, conv_b_dw, conv_ln_g, conv_ln_b, conv_w_pw2, conv_b_pw2, pool_w, pool_scale, fnet_w, fnet_b, final_norm_g):
    depth = w_mod.shape[0]
    x = jnp.concatenate([x_prompt.reshape(PROMPT_ROWS, D), x_sample.reshape(N_SAMPLE_SEQ * T_SAMPLE, D)], axis=0)
    cond = jnp.concatenate([c_ctx[None, :], c, jnp.zeros((MOD_ROWS - N_GROUPS, D), F32)], axis=0)
    mod_all = _adaln_all(cond, w_mod, b_mod).reshape(depth, MOD_ROWS, 1, 6 * D)
    final_g = final_norm_g.reshape(1, D)
    new_f, new_b = [], []
    y_prompt = y_sample = None
    for i in range(depth):
        kind, j = i % 4, i // 4
        mod = mod_all[i]
        g_mix = norm_mix_g[i].reshape(1, D)
        if kind == 0:
            x, sf, sb = _retention_layer(x, mod, g_mix, ret_w_in[j], ret_gn_g[j], ret_w_out[j],
                                         ret_decay_fwd[j], ret_decay_bwd[j],
                                         state_ret_fwd[:, j], state_ret_bwd[:, j])
            new_f.append(sf)
            new_b.append(sb)
        elif kind == 1:
            x = _conv_layer(x, mod, g_mix, conv_w_pw1[j], conv_b_pw1[j], conv_w_dw[j], conv_b_dw[j],
                            conv_ln_g[j], conv_ln_b[j], conv_w_pw2[j], conv_b_pw2[j])
        elif kind == 2:
            x = _pool_layer(x, mod, g_mix, pool_w[j], pool_scale[j])
        else:
            x = _fourier_layer(x, mod, g_mix, fnet_w[j], fnet_b[j])
        g_mlp = norm_mlp_g[i].reshape(1, D)
        w1, w2 = mlp_w1[i].astype(BF16), mlp_w2[i].astype(BF16)
        if i == depth - 1:
            npt = PROMPT_ROWS // TM
            y_prompt = _mlp(x, mod, g_mlp, w1, w2, final_g, 0, npt, True, "mlp_final_prompt")
            y_sample = _mlp(x, mod, g_mlp, w1, w2, final_g, npt, N_ROWS // TM - npt, True, "mlp_final_sample")
        else:
            x = _mlp(x, mod, g_mlp, w1, w2, final_g, 0, N_ROWS // TM, False, "mlp")
    return (y_prompt.reshape(N_PROMPT_SEQ, T_PROMPT, D),
            y_sample.reshape(N_SAMPLE_SEQ, T_SAMPLE, D),
            jnp.stack(new_f, axis=1),
            jnp.stack(new_b, axis=1))
```

```python
import functools
import math

import numpy as np
import jax
import jax.numpy as jnp
from jax import lax
from jax.experimental import pallas as pl
from jax.experimental.pallas import tpu as pltpu

F32 = jnp.float32
BF16 = jnp.bfloat16

D = 1024
D_FF = 4 * D
N_PROMPT_SEQ = 16
T_PROMPT = 256
N_SAMPLE_SEQ = 4
T_SAMPLE = 4096
GROUP_ROWS = 4096
N_GROUPS = 1 + N_SAMPLE_SEQ
N_ROWS = N_GROUPS * GROUP_ROWS
PROMPT_ROWS = N_PROMPT_SEQ * T_PROMPT
MOD_ROWS = 16
GRID_W = 64
HEADS = 4
DK = 256
DV = 512
HK = HEADS * DK
HV = HEADS * DV
RET_CHUNK = 256
ROPE_BASE = 10000.0
CONV_WIDTH = 31
CONV_PAD = CONV_WIDTH // 2
CONV_HALO = 16
POOL_WINDOWS = (2, 4, 8, 16)
POOL_HALO = 8
GW = 256
N_CGROUPS = D // GW
NORM_EPS = 1e-6
GN_EPS = 1e-5

TM = 1024
TILES_PER_GROUP = GROUP_ROWS // TM
TS = 256
VMEM_LIMIT = 56 * 1024 * 1024


def _cparams(sem):
    return pltpu.CompilerParams(dimension_semantics=sem, vmem_limit_bytes=VMEM_LIMIT)


def _norm_mod(x, g, shift, scale):
    ms = jnp.mean(x * x, axis=-1, keepdims=True)
    y = (x * lax.rsqrt(ms + NORM_EPS)) * g
    return y * (1.0 + scale) + shift


def _mod_slice(mod_ref, idx):
    return mod_ref[:, idx * D:(idx + 1) * D]


def _silu(x):
    return x * jax.nn.sigmoid(x)


def _mod_kernel(cond_ref, w_ref, b_ref, o_ref):
    a = _silu(cond_ref[...]).astype(BF16)
    o_ref[...] = jnp.dot(a, w_ref[...].astype(BF16), preferred_element_type=F32) + b_ref[...]


def _adaln_all(cond, w_mod, b_mod):
    depth = w_mod.shape[0]
    tn = 1536
    return pl.pallas_call(
        _mod_kernel,
        out_shape=jax.ShapeDtypeStruct((depth, MOD_ROWS, 6 * D), F32),
        grid=(depth, 6 * D // tn),
        in_specs=[
            pl.BlockSpec((MOD_ROWS, D), lambda l, j: (0, 0)),
            pl.BlockSpec((None, D, tn), lambda l, j: (l, 0, j)),
            pl.BlockSpec((None, 1, tn), lambda l, j: (l, 0, j)),
        ],
        out_specs=pl.BlockSpec((None, MOD_ROWS, tn), lambda l, j: (l, 0, j)),
        compiler_params=_cparams(("parallel", "parallel")),
        name="adaln_params",
    )(cond, w_mod, b_mod.reshape(depth, 1, 6 * D))


def _mod_spec():
    return pl.BlockSpec((None, 1, 6 * D), lambda i, j: (i // TILES_PER_GROUP, 0, 0))


def _nm_prologue(x_ref, mod_ref, g_ref, h_scr, shift_idx):
    @pl.when(pl.program_id(1) == 0)
    def _():
        h = _norm_mod(x_ref[...], g_ref[...], _mod_slice(mod_ref, shift_idx), _mod_slice(mod_ref, shift_idx + 1))
        h_scr[...] = h.astype(BF16)


def _nm_glu_kernel(x_ref, mod_ref, g_ref, wa_ref, wg_ref, ba_ref, bg_ref, o_ref, h_scr, *, shift_idx):
    _nm_prologue(x_ref, mod_ref, g_ref, h_scr, shift_idx)
    h = h_scr[...]
    a = jnp.dot(h, wa_ref[...], preferred_element_type=F32) + ba_ref[...]
    gt = jnp.dot(h, wg_ref[...], preferred_element_type=F32) + bg_ref[...]
    o_ref[...] = (a * jax.nn.sigmoid(gt)).astype(o_ref.dtype)


def _nm_call(kernel, x, mod, g, extra_in, extra_specs, n_cols, tn, out_dtype, name):
    return pl.pallas_call(
        kernel,
        out_shape=jax.ShapeDtypeStruct((N_ROWS, n_cols), out_dtype),
        grid=(N_ROWS // TM, n_cols // tn),
        in_specs=[
            pl.BlockSpec((TM, D), lambda i, j: (i, 0)),
            _mod_spec(),
            pl.BlockSpec((1, D), lambda i, j: (0, 0)),
        ] + extra_specs,
        out_specs=pl.BlockSpec((TM, tn), lambda i, j: (i, j)),
        scratch_shapes=[pltpu.VMEM((TM, D), BF16)],
        compiler_params=_cparams(("parallel", "arbitrary")),
        name=name,
    )(x, mod, g, *extra_in)


def _nm_glu(x, mod, g, w, b, shift_idx, tn):
    off = D // tn
    b2 = b.reshape(1, 2 * D)
    return _nm_call(functools.partial(_nm_glu_kernel, shift_idx=shift_idx), x, mod, g,
                    [w, w, b2, b2],
                    [pl.BlockSpec((D, tn), lambda i, j: (0, j)),
                     pl.BlockSpec((D, tn), lambda i, j: (0, j + off)),
                     pl.BlockSpec((1, tn), lambda i, j: (0, j)),
                     pl.BlockSpec((1, tn), lambda i, j: (0, j + off))],
                    D, tn, F32, "conv_pw1_glu")


def _out_kernel(yp_ref, ys_ref, w_ref, b_ref, xp_ref, *rest, gate_idx, n_prompt_tiles):
    xs_ref = rest[0] if len(rest) == 3 else xp_ref
    mod_ref, o_ref = rest[-2:]
    i = pl.program_id(0)
    w = w_ref[...]

    def finish(y_ref, x_ref):
        acc = jnp.dot(y_ref[...], w, preferred_element_type=F32) + b_ref[...]
        o_ref[...] = x_ref[...] + _mod_slice(mod_ref, gate_idx) * acc

    @pl.when(i < n_prompt_tiles)
    def _():
        finish(yp_ref, xp_ref)

    @pl.when(i >= n_prompt_tiles)
    def _():
        finish(ys_ref, xs_ref)


def _out_proj(y_prompt, y_sample, w, b, x, mod, gate_idx, name):
    k = w.shape[0]
    npt = PROMPT_ROWS // TM
    prompt_map = lambda i: (jnp.minimum(i, npt - 1), 0)
    sample_map = lambda i: (jnp.maximum(i - npt, 0), 0)
    if isinstance(x, tuple):
        x_args = list(x)
        x_specs = [pl.BlockSpec((TM, D), prompt_map), pl.BlockSpec((TM, D), sample_map)]
    else:
        x_args = [x]
        x_specs = [pl.BlockSpec((TM, D), lambda i: (i, 0))]
    return pl.pallas_call(
        functools.partial(_out_kernel, gate_idx=gate_idx, n_prompt_tiles=npt),
        out_shape=jax.ShapeDtypeStruct((N_ROWS, D), F32),
        grid=(N_ROWS // TM,),
        in_specs=[
            pl.BlockSpec((TM, k), prompt_map),
            pl.BlockSpec((TM, k), sample_map),
            pl.BlockSpec((k, D), lambda i: (0, 0)),
            pl.BlockSpec((1, D), lambda i: (0, 0)),
        ] + x_specs + [
            pl.BlockSpec((None, 1, 6 * D), lambda i: (i // TILES_PER_GROUP, 0, 0)),
        ],
        out_specs=pl.BlockSpec((TM, D), lambda i: (i, 0)),
        compiler_params=_cparams(("parallel",)),
        name=name,
    )(y_prompt, y_sample, w, b.reshape(1, D), *x_args, mod)


def _mlp_kernel(x_ref, mod_ref, g_ref, w1_ref, w2_ref, fg_ref, o_ref, h_scr, acc_scr, *, n_ff, final_norm):
    f = pl.program_id(1)

    @pl.when(f == 0)
    def _():
        h = _norm_mod(x_ref[...], g_ref[...], _mod_slice(mod_ref, 3), _mod_slice(mod_ref, 4))
        h_scr[...] = h.astype(BF16)
        acc_scr[...] = jnp.zeros_like(acc_scr)

    hid = jnp.maximum(jnp.dot(h_scr[...], w1_ref[...], preferred_element_type=F32), 0.0)
    acc_scr[...] += jnp.dot((hid * hid).astype(BF16), w2_ref[...], preferred_element_type=F32)

    @pl.when(f == n_ff - 1)
    def _():
        y = x_ref[...] + _mod_slice(mod_ref, 5) * acc_scr[...]
        if final_norm:
            ms = jnp.mean(y * y, axis=-1, keepdims=True)
            y = (y * lax.rsqrt(ms + NORM_EPS)) * fg_ref[...]
        o_ref[...] = y


def _mlp(x, mod, g, w1, w2, final_g, row_tile0, n_tiles, final_norm, name):
    tf = 1024
    n_ff = D_FF // tf
    return pl.pallas_call(
        functools.partial(_mlp_kernel, n_ff=n_ff, final_norm=final_norm),
        out_shape=jax.ShapeDtypeStruct((n_tiles * TM, D), F32),
        grid=(n_tiles, n_ff),
        in_specs=[
            pl.BlockSpec((TM, D), lambda i, f: (i + row_tile0, 0)),
            pl.BlockSpec((None, 1, 6 * D), lambda i, f: ((i + row_tile0) // TILES_PER_GROUP, 0, 0)),
            pl.BlockSpec((1, D), lambda i, f: (0, 0)),
            pl.BlockSpec((D, tf), lambda i, f: (0, f)),
            pl.BlockSpec((tf, D), lambda i, f: (f, 0)),
            pl.BlockSpec((1, D), lambda i, f: (0, 0)),
        ],
        out_specs=pl.BlockSpec((TM, D), lambda i, f: (i, 0)),
        scratch_shapes=[pltpu.VMEM((TM, D), BF16), pltpu.VMEM((TM, D), F32)],
        compiler_params=_cparams(("parallel", "arbitrary")),
        name=name,
    )(x, mod, g, w1, w2, final_g)


def _scan_kernel(*refs, chunk, n_chunks, reverse, has_s0, has_zin, emit_state):
    refs = list(refs)
    p_ref, q_ref, k_ref, v_ref, g_ref, gn_ref = refs[:6]
    pos = 6
    s0_ref = zin_ref = sout_ref = None
    if has_s0:
        s0_ref = refs[pos]; pos += 1
    if has_zin:
        zin_ref = refs[pos]; pos += 1
    z_ref = refs[pos]; pos += 1
    if emit_state:
        sout_ref = refs[pos]; pos += 1
    s_scr, d_scr, xi_scr, zeta_scr = refs[pos:pos + 4]

    c = pl.program_id(1)
    log_g = [jnp.log1p(-jnp.exp2(-p_ref[h]))[:, :1] for h in range(HEADS)]

    @pl.when(c == 0)
    def _():
        ri = lax.broadcasted_iota(jnp.int32, (chunk, chunk), 0)
        ci = lax.broadcasted_iota(jnp.int32, (chunk, chunk), 1)
        rel = (ci - ri) if reverse else (ri - ci)
        relf = jnp.maximum(rel, 0).astype(F32)
        t = lax.broadcasted_iota(jnp.int32, (chunk, 128), 0)
        step = ((chunk - 1) - t if reverse else t).astype(F32)
        for h in range(HEADS):
            d_scr[h] = jnp.where(rel >= 0, jnp.exp(log_g[h] * relf), 0.0)
            xi_scr[h] = jnp.exp(log_g[h] * (step + 1.0))
            zeta_scr[h] = jnp.exp(log_g[h] * ((chunk - 1.0) - step))
        if has_s0:
            s_scr[...] = s0_ref[...]
        else:
            s_scr[...] = jnp.zeros_like(s_scr)

    for h in range(HEADS):
        kcols = slice(h * DK, (h + 1) * DK)
        vcols = slice(h * DV, (h + 1) * DV)
        q = q_ref[:, kcols]
        k = k_ref[:, kcols]
        v = v_ref[:, vcols]
        s = s_scr[h]
        scores = lax.dot_general(q, k, (((1,), (1,)), ((), ())), preferred_element_type=F32)
        inner = jnp.dot((scores * d_scr[h]).astype(BF16), v, preferred_element_type=F32)
        cross = jnp.dot(q, s.astype(BF16), preferred_element_type=F32)
        xi = jnp.concatenate([xi_scr[h]] * (DV // 128), axis=-1)
        o = inner + cross * xi
        zeta = jnp.concatenate([zeta_scr[h]] * (DK // 128), axis=-1)
        kz = (k.astype(F32) * zeta).astype(BF16)
        upd = lax.dot_general(kz, v, (((0,), (0,)), ((), ())), preferred_element_type=F32)
        s_scr[h] = s * jnp.exp(log_g[h] * float(chunk)) + upd

        mu = jnp.mean(o, axis=-1, keepdims=True)
        dlt = o - mu
        var = jnp.mean(dlt * dlt, axis=-1, keepdims=True)
        z = _silu(g_ref[:, vcols].astype(F32)) * ((dlt * lax.rsqrt(var + GN_EPS)) * gn_ref[:, vcols])
        if has_zin:
            z = zin_ref[:, vcols] + z
        z_ref[:, vcols] = z.astype(z_ref.dtype)

    if emit_state:
        @pl.when(c == n_chunks - 1)
        def _():
            sout_ref[...] = s_scr[...]


def _scan(decay_p, proj, gn_g, s0, zin, *, row0, n_seq, seq_len, reverse, emit_state, out_dtype, name):
    chunk = RET_CHUNK
    nc = seq_len // chunk
    rb0 = row0 // chunk

    def rb(b, c):
        cc = (nc - 1 - c) if reverse else c
        return b * nc + cc

    in_specs = [
        pl.BlockSpec((HEADS, 1, 128), lambda b, c: (0, 0, 0)),
        pl.BlockSpec((chunk, HK), lambda b, c: (rb0 + rb(b, c), 0)),
        pl.BlockSpec((chunk, HK), lambda b, c: (rb0 + rb(b, c), 1)),
        pl.BlockSpec((chunk, HV), lambda b, c: (rb0 + rb(b, c), 1)),
        pl.BlockSpec((chunk, HV), lambda b, c: (rb0 + rb(b, c), 3 if reverse else 2)),
        pl.BlockSpec((1, HV), lambda b, c: (0, 0)),
    ]
    args = [decay_p, proj, proj, proj, proj, gn_g]
    if s0 is not None:
        in_specs.append(pl.BlockSpec((None, HEADS, DK, DV), lambda b, c: (b, 0, 0, 0)))
        args.append(s0)
    if zin is not None:
        in_specs.append(pl.BlockSpec((chunk, HV), lambda b, c: (rb(b, c), 0)))
        args.append(zin)
    out_shape = [jax.ShapeDtypeStruct((n_seq * seq_len, HV), out_dtype)]
    out_specs = [pl.BlockSpec((chunk, HV), lambda b, c: (rb(b, c), 0))]
    if emit_state:
        out_shape.append(jax.ShapeDtypeStruct((n_seq, HEADS, DK, DV), F32))
        out_specs.append(pl.BlockSpec((None, HEADS, DK, DV), lambda b, c: (b, 0, 0, 0)))
    res = pl.pallas_call(
        functools.partial(_scan_kernel, chunk=chunk, n_chunks=nc, reverse=reverse,
                          has_s0=s0 is not None, has_zin=zin is not None, emit_state=emit_state),
        out_shape=out_shape,
        grid=(n_seq, nc),
        in_specs=in_specs,
        out_specs=out_specs,
        scratch_shapes=[pltpu.VMEM((HEADS, DK, DV), F32), pltpu.VMEM((HEADS, chunk, chunk), F32),
                        pltpu.VMEM((HEADS, chunk, 128), F32), pltpu.VMEM((HEADS, chunk, 128), F32)],
        compiler_params=_cparams(("parallel", "arbitrary")),
        name=name,
    )(*args)
    return res if emit_state else (res[0], None)


def _rope_tables():
    n = DK // 4
    inv = ROPE_BASE ** (-jnp.arange(n, dtype=F32) / n)
    t = jnp.arange(T_SAMPLE, dtype=jnp.int32)
    row = (t // GRID_W).astype(F32)
    col = (t % GRID_W).astype(F32)
    ang = jnp.concatenate([row[:, None] * inv[None, :], col[:, None] * inv[None, :]], axis=-1)
    cos = jnp.concatenate([jnp.ones((PROMPT_ROWS, DK // 2), F32)] + [jnp.cos(ang)] * N_SAMPLE_SEQ, axis=0)
    sin = jnp.concatenate([jnp.zeros((PROMPT_ROWS, DK // 2), F32)] + [jnp.sin(ang)] * N_SAMPLE_SEQ, axis=0)
    return cos, sin


def _ret_proj_kernel(xp_ref, xs_ref, mod_ref, g_ref, w_ref, cos_ref, sin_ref, o_ref, h_scr, *, n_prompt_tiles):
    i = pl.program_id(0)
    j = pl.program_id(1)

    def prologue(x_ref):
        h = _norm_mod(x_ref[...], g_ref[...], _mod_slice(mod_ref, 0), _mod_slice(mod_ref, 1))
        h_scr[...] = h.astype(BF16)

    @pl.when((j == 0) & (i < n_prompt_tiles))
    def _():
        prologue(xp_ref)

    @pl.when((j == 0) & (i >= n_prompt_tiles))
    def _():
        prologue(xs_ref)

    acc = jnp.dot(h_scr[...], w_ref[...], preferred_element_type=F32)

    @pl.when(j < 2)
    def _():
        half = DK // 2
        cos, sin = cos_ref[...], sin_ref[...]
        kscale = jnp.where(j == 1, DK ** -0.5, 1.0).astype(F32)
        for h in range(HEADS):
            x1 = acc[:, h * DK:h * DK + half]
            x2 = acc[:, h * DK + half:(h + 1) * DK]
            o_ref[:, h * DK:h * DK + half] = ((x1 * cos - x2 * sin) * kscale).astype(o_ref.dtype)
            o_ref[:, h * DK + half:(h + 1) * DK] = ((x2 * cos + x1 * sin) * kscale).astype(o_ref.dtype)

    @pl.when(j >= 2)
    def _():
        o_ref[...] = acc.astype(o_ref.dtype)


def _ret_proj(xp, xs, mod, g, w_in, cos, sin):
    npt = PROMPT_ROWS // TM
    half = DK // 2
    n_cols = w_in.shape[1]
    return pl.pallas_call(
        functools.partial(_ret_proj_kernel, n_prompt_tiles=npt),
        out_shape=jax.ShapeDtypeStruct((N_ROWS, n_cols), BF16),
        grid=(N_ROWS // TM, n_cols // HK),
        in_specs=[
            pl.BlockSpec((TM, D), lambda i, j: (jnp.minimum(i, npt - 1), 0)),
            pl.BlockSpec((TM, D), lambda i, j: (jnp.maximum(i - npt, 0), 0)),
            _mod_spec(),
            pl.BlockSpec((1, D), lambda i, j: (0, 0)),
            pl.BlockSpec((D, HK), lambda i, j: (0, j)),
            pl.BlockSpec((TM, half), lambda i, j: (i, 0)),
            pl.BlockSpec((TM, half), lambda i, j: (i, 0)),
        ],
        out_specs=pl.BlockSpec((TM, HK), lambda i, j: (i, j)),
        scratch_shapes=[pltpu.VMEM((TM, D), BF16)],
        compiler_params=_cparams(("parallel", "arbitrary")),
        name="ret_in_proj",
    )(xp, xs, mod, g, w_in, cos, sin)


def _retention_layer(xp, xs, mod, g, w_in, gn_g, w_out, decay_fwd, decay_bwd, s0_fwd, s0_bwd):
    cos, sin = _rope_tables()
    proj = _ret_proj(xp, xs, mod, g, w_in.astype(BF16), cos, sin)
    gn = gn_g.reshape(1, HV)
    pf = jnp.broadcast_to(decay_fwd.astype(F32)[:, None, None], (HEADS, 1, 128))
    pb = jnp.broadcast_to(decay_bwd.astype(F32)[:, None, None], (HEADS, 1, 128))
    prompt = dict(row0=0, n_seq=N_PROMPT_SEQ, seq_len=T_PROMPT, emit_state=True)
    sample = dict(row0=PROMPT_ROWS, n_seq=N_SAMPLE_SEQ, seq_len=T_SAMPLE, emit_state=False)
    zp, sf = _scan(pf, proj, gn, None, None, reverse=False, out_dtype=F32, name="ret_scan_prompt_fwd", **prompt)
    yp, sb = _scan(pb, proj, gn, None, zp, reverse=True, out_dtype=BF16, name="ret_scan_prompt_bwd", **prompt)
    zs, _ = _scan(pf, proj, gn, s0_fwd, None, reverse=False, out_dtype=F32, name="ret_scan_sample_fwd", **sample)
    ys, _ = _scan(pb, proj, gn, s0_bwd, zs, reverse=True, out_dtype=BF16, name="ret_scan_sample_bwd", **sample)
    x = _out_proj(yp, ys, w_out.astype(BF16), jnp.zeros((D,), F32), (xp, xs), mod, 2, "ret_out_proj")
    return x, sf, sb


def _seq_tile_flags(i):
    n_prompt_tiles = PROMPT_ROWS // TS
    tiles_per_seq = T_SAMPLE // TS
    is_sample = i >= n_prompt_tiles
    tin = (i - n_prompt_tiles) % tiles_per_seq
    if T_PROMPT != TS:
        raise NotImplementedError("prompt sequences must be exactly one row tile")
    return is_sample & (tin > 0), is_sample & (tin < tiles_per_seq - 1)


def _conv_kernel(u_ref, up_ref, un_ref, x_ref, mod_ref, wdw_ref, bdw_ref, lng_ref, lnb_ref, w2_ref, b2_ref,
                 o_ref, ext_scr, cv_scr):
    has_prev, has_next = _seq_tile_flags(pl.program_id(0))
    n_slabs = D // 128
    for lt in range(n_slabs):
        lanes = slice(lt * 128, (lt + 1) * 128)
        ext_scr[lt, 0:CONV_HALO, :] = jnp.where(has_prev, up_ref[:, lanes], 0.0)
        ext_scr[lt, CONV_HALO:CONV_HALO + TS, :] = u_ref[:, lanes]
        ext_scr[lt, CONV_HALO + TS:, :] = jnp.where(has_next, un_ref[:, lanes], 0.0)

    rows = 32
    shift = CONV_HALO - CONV_PAD

    def body(lt, carry):
        for r0 in range(0, TS, rows):
            acc = jnp.broadcast_to(bdw_ref[lt], (rows, 128))
            for kk in range(CONV_WIDTH):
                acc = acc + wdw_ref[lt, kk:kk + 1, :] * ext_scr[lt, r0 + kk + shift:r0 + kk + shift + rows, :]
            cv_scr[lt, r0:r0 + rows, :] = acc
        return carry

    lax.fori_loop(0, n_slabs, body, 0)

    cv = jnp.concatenate([cv_scr[lt] for lt in range(n_slabs)], axis=-1)
    mu = jnp.mean(cv, axis=-1, keepdims=True)
    dlt = cv - mu
    var = jnp.mean(dlt * dlt, axis=-1, keepdims=True)
    ln = (dlt * lax.rsqrt(var + GN_EPS)) * lng_ref[...] + lnb_ref[...]
    act = _silu(ln).astype(BF16)
    y = jnp.dot(act, w2_ref[...], preferred_element_type=F32) + b2_ref[...]
    o_ref[...] = x_ref[...] + _mod_slice(mod_ref, 2) * y


def _conv_layer(x, mod, g, w_pw1, b_pw1, w_dw, b_dw, ln_g, ln_b, w_pw2, b_pw2):
    u = _nm_glu(x, mod, g, w_pw1.astype(BF16), b_pw1, 0, 512)
    hb = TS // CONV_HALO
    last = N_ROWS // CONV_HALO - 1
    row = lambda a: a.reshape(1, D)
    return pl.pallas_call(
        _conv_kernel,
        out_shape=jax.ShapeDtypeStruct((N_ROWS, D), F32),
        grid=(N_ROWS // TS,),
        in_specs=[
            pl.BlockSpec((TS, D), lambda i: (i, 0)),
            pl.BlockSpec((CONV_HALO, D), lambda i: (jnp.maximum(i * hb - 1, 0), 0)),
            pl.BlockSpec((CONV_HALO, D), lambda i: (jnp.minimum((i + 1) * hb, last), 0)),
            pl.BlockSpec((TS, D), lambda i: (i, 0)),
            pl.BlockSpec((None, 1, 6 * D), lambda i: (i // (GROUP_ROWS // TS), 0, 0)),
            pl.BlockSpec((D // 128, CONV_WIDTH, 128), lambda i: (0, 0, 0)),
            pl.BlockSpec((D // 128, 1, 128), lambda i: (0, 0, 0)),
            pl.BlockSpec((1, D), lambda i: (0, 0)),
            pl.BlockSpec((1, D), lambda i: (0, 0)),
            pl.BlockSpec((D, D), lambda i: (0, 0)),
            pl.BlockSpec((1, D), lambda i: (0, 0)),
        ],
        out_specs=pl.BlockSpec((TS, D), lambda i: (i, 0)),
        scratch_shapes=[pltpu.VMEM((D // 128, TS + 2 * CONV_HALO, 128), F32), pltpu.VMEM((D // 128, TS, 128), F32)],
        compiler_params=_cparams(("parallel",)),
        name="conv_dw_ln_pw2",
    )(u, u, u, x, mod, w_dw.reshape(CONV_WIDTH, D // 128, 128).transpose(1, 0, 2), b_dw.reshape(D // 128, 1, 128),
      row(ln_g), row(ln_b), w_pw2.astype(BF16), row(b_pw2))


def _pool_kernel(x_ref, xp_ref, xn_ref, mod_ref, g_ref, w_ref, sc_ref, o_ref, ext_scr):
    i = pl.program_id(0)
    has_prev, has_next = _seq_tile_flags(i)
    shift, scale = _mod_slice(mod_ref, 0), _mod_slice(mod_ref, 1)
    g = g_ref[...]
    x = x_ref[...]
    h = _norm_mod(x, g, shift, scale)
    ext_scr[0:POOL_HALO, :] = jnp.where(has_prev, _norm_mod(xp_ref[...], g, shift, scale), 0.0)
    ext_scr[POOL_HALO:POOL_HALO + TS, :] = h
    ext_scr[POOL_HALO + TS:, :] = jnp.where(has_next, _norm_mod(xn_ref[...], g, shift, scale), 0.0)

    seq_len = jnp.where(i >= PROMPT_ROWS // TS, T_SAMPLE, T_PROMPT)
    t = (i * TS) % seq_len + lax.broadcasted_iota(jnp.int32, (TS, GW), 0)
    outs = []
    for gi, w in enumerate(POOL_WINDOWS):
        lanes = slice(gi * GW, (gi + 1) * GW)
        tot = ext_scr[pl.ds(POOL_HALO - w // 2, TS), lanes]
        for j in range(1, w):
            tot = tot + ext_scr[pl.ds(POOL_HALO - w // 2 + j, TS), lanes]
        cnt = jnp.minimum(t + w // 2, seq_len) - jnp.maximum(t - w // 2, 0)
        p = tot / cnt.astype(F32) - h[:, lanes]
        outs.append(jnp.dot(p.astype(BF16), w_ref[gi], preferred_element_type=F32))
    y = jnp.concatenate(outs, axis=-1) * sc_ref[...]
    o_ref[...] = x + _mod_slice(mod_ref, 2) * y


def _pool_layer(x, mod, g, w_grp, scale):
    hb = TS // POOL_HALO
    last = N_ROWS // POOL_HALO - 1
    return pl.pallas_call(
        _pool_kernel,
        out_shape=jax.ShapeDtypeStruct((N_ROWS, D), F32),
        grid=(N_ROWS // TS,),
        in_specs=[
            pl.BlockSpec((TS, D), lambda i: (i, 0)),
            pl.BlockSpec((POOL_HALO, D), lambda i: (jnp.maximum(i * hb - 1, 0), 0)),
            pl.BlockSpec((POOL_HALO, D), lambda i: (jnp.minimum((i + 1) * hb, last), 0)),
            pl.BlockSpec((None, 1, 6 * D), lambda i: (i // (GROUP_ROWS // TS), 0, 0)),
            pl.BlockSpec((1, D), lambda i: (0, 0)),
            pl.BlockSpec((N_CGROUPS, GW, GW), lambda i: (0, 0, 0)),
            pl.BlockSpec((1, D), lambda i: (0, 0)),
        ],
        out_specs=pl.BlockSpec((TS, D), lambda i: (i, 0)),
        scratch_shapes=[pltpu.VMEM((TS + 2 * POOL_HALO, D), F32)],
        compiler_params=_cparams(("parallel",)),
        name="pool_mixer",
    )(x, x, x, mod, g, w_grp.astype(BF16), scale.reshape(1, D))


def _chan_dft_kernel(x_ref, mod_ref, g_ref, c_ref, s_ref, a_ref, b_ref):
    h = _norm_mod(x_ref[...], g_ref[...], _mod_slice(mod_ref, 0), _mod_slice(mod_ref, 1)).astype(BF16)
    c, s = c_ref[...], s_ref[...]
    for gi in range(N_CGROUPS):
        lanes = slice(gi * GW, (gi + 1) * GW)
        a_ref[:, lanes] = jnp.dot(h[:, lanes], c, preferred_element_type=F32).astype(a_ref.dtype)
        b_ref[:, lanes] = jnp.dot(h[:, lanes], s, preferred_element_type=F32).astype(b_ref.dtype)


def _dft_tables(n):
    idx = jnp.arange(n, dtype=jnp.int32)
    ang = (2.0 * np.pi / n) * ((idx[:, None] * idx[None, :]) % n).astype(F32)
    return jnp.cos(ang), jnp.sin(ang)


def _chan_dft(x, mod, g):
    c, s = _dft_tables(GW)
    return pl.pallas_call(
        _chan_dft_kernel,
        out_shape=[jax.ShapeDtypeStruct((N_ROWS, D), BF16)] * 2,
        grid=(N_ROWS // TM,),
        in_specs=[
            pl.BlockSpec((TM, D), lambda i: (i, 0)),
            pl.BlockSpec((None, 1, 6 * D), lambda i: (i // TILES_PER_GROUP, 0, 0)),
            pl.BlockSpec((1, D), lambda i: (0, 0)),
            pl.BlockSpec((GW, GW), lambda i: (0, 0)),
            pl.BlockSpec((GW, GW), lambda i: (0, 0)),
        ],
        out_specs=[pl.BlockSpec((TM, D), lambda i: (i, 0))] * 2,
        compiler_params=_cparams(("parallel",)),
        name="fnet_chan_dft",
    )(x, mod, g, c.astype(BF16), s.astype(BF16))


def _time_dft_kernel(c_ref, sn_ref, a_ref, b_ref, o_ref, acc_scr, *, n_k, scale):
    k = pl.program_id(2)

    @pl.when(k == 0)
    def _():
        acc_scr[...] = jnp.zeros_like(acc_scr)

    acc_scr[...] += (jnp.dot(c_ref[...], a_ref[...], preferred_element_type=F32)
                     + jnp.dot(sn_ref[...], b_ref[...], preferred_element_type=F32))

    @pl.when(k == n_k - 1)
    def _():
        o_ref[...] = (acc_scr[...] * scale).astype(o_ref.dtype)


def _time_dft(cos_t, nsin_t, a, b, *, row0, n_seq, seq_len, name):
    tm = min(seq_len, 1024)
    tk = min(seq_len, 512)
    n_i, n_k = seq_len // tm, seq_len // tk
    kb0 = row0 // tk
    scale = 1.0 / math.sqrt(seq_len * GW)
    return pl.pallas_call(
        functools.partial(_time_dft_kernel, n_k=n_k, scale=scale),
        out_shape=jax.ShapeDtypeStruct((n_seq * seq_len, D), BF16),
        grid=(n_seq, n_i, n_k),
        in_specs=[
            pl.BlockSpec((tm, tk), lambda s, i, k: (i, k)),
            pl.BlockSpec((tm, tk), lambda s, i, k: (i, k)),
            pl.BlockSpec((tk, D), lambda s, i, k: (kb0 + s * n_k + k, 0)),
            pl.BlockSpec((tk, D), lambda s, i, k: (kb0 + s * n_k + k, 0)),
        ],
        out_specs=pl.BlockSpec((tm, D), lambda s, i, k: (s * n_i + i, 0)),
        scratch_shapes=[pltpu.VMEM((tm, D), F32)],
        compiler_params=_cparams(("parallel", "parallel", "arbitrary")),
        name=name,
    )(cos_t, nsin_t, a, b)


def _big_dft_tables(n, n1):
    n2 = n // n1
    f = jnp.arange(n, dtype=jnp.int32)[:, None]
    ang1 = (2.0 * np.pi / n1) * ((f * jnp.arange(n1, dtype=jnp.int32)[None, :]) % n1).astype(F32)
    ang2 = (2.0 * np.pi / n) * ((f * jnp.arange(n2, dtype=jnp.int32)[None, :]) % n).astype(F32)
    c1, s1 = jnp.cos(ang1)[:, :, None], jnp.sin(ang1)[:, :, None]
    c2, s2 = jnp.cos(ang2)[:, None, :], jnp.sin(ang2)[:, None, :]
    cos_t = (c1 * c2 - s1 * s2).reshape(n, n)
    sin_t = (s1 * c2 + c1 * s2).reshape(n, n)
    return cos_t.astype(BF16), (-sin_t).astype(BF16)


def _fourier_layer(x, mod, g, w, b):
    a, bb = _chan_dft(x, mod, g)
    cp, sp = _dft_tables(T_PROMPT)
    fp = _time_dft(cp.astype(BF16), (-sp).astype(BF16), a, bb,
                   row0=0, n_seq=N_PROMPT_SEQ, seq_len=T_PROMPT, name="fnet_time_dft_prompt")
    cs, ss = _big_dft_tables(T_SAMPLE, GRID_W)
    fs = _time_dft(cs, ss, a, bb, row0=PROMPT_ROWS, n_seq=N_SAMPLE_SEQ, seq_len=T_SAMPLE,
                   name="fnet_time_dft_sample")
    return _out_proj(fp, fs, w.astype(BF16), b, x, mod, 2, "fnet_out_proj")


def kernel(x_prompt, x_sample, state_ret_fwd, state_ret_bwd, c, c_ctx, w_mod, b_mod, norm_mix_g, norm_mlp_g, mlp_w1, mlp_w2, ret_w_in, ret_gn_g, ret_w_out, ret_decay_fwd, ret_decay_bwd, conv_w_pw1, conv_b_pw1, conv_w_dw, conv_b_dw, conv_ln_g, conv_ln_b, conv_w_pw2, conv_b_pw2, pool_w, pool_scale, fnet_w, fnet_b, final_norm_g):
    depth = w_mod.shape[0]
    x = (x_prompt.reshape(PROMPT_ROWS, D), x_sample.reshape(N_SAMPLE_SEQ * T_SAMPLE, D))
    cond = jnp.concatenate([c_ctx[None, :], c, jnp.zeros((MOD_ROWS - N_GROUPS, D), F32)], axis=0)
    mod_all = _adaln_all(cond, w_mod, b_mod).reshape(depth, MOD_ROWS, 1, 6 * D)
    final_g = final_norm_g.reshape(1, D)
    new_f, new_b = [], []
    y_prompt = y_sample = None
    for i in range(depth):
        kind, j = i % 4, i // 4
        mod = mod_all[i]
        g_mix = norm_mix_g[i].reshape(1, D)
        if kind != 0 and isinstance(x, tuple):
            x = jnp.concatenate(x, axis=0)
        if kind == 0:
            xp, xs = x if isinstance(x, tuple) else (x[:PROMPT_ROWS], x[PROMPT_ROWS:])
            x, sf, sb = _retention_layer(xp, xs, mod, g_mix, ret_w_in[j], ret_gn_g[j], ret_w_out[j],
                                         ret_decay_fwd[j], ret_decay_bwd[j],
                                         state_ret_fwd[:, j], state_ret_bwd[:, j])
            new_f.append(sf)
            new_b.append(sb)
        elif kind == 1:
            x = _conv_layer(x, mod, g_mix, conv_w_pw1[j], conv_b_pw1[j], conv_w_dw[j], conv_b_dw[j],
                            conv_ln_g[j], conv_ln_b[j], conv_w_pw2[j], conv_b_pw2[j])
        elif kind == 2:
            x = _pool_layer(x, mod, g_mix, pool_w[j], pool_scale[j])
        else:
            x = _fourier_layer(x, mod, g_mix, fnet_w[j], fnet_b[j])
        g_mlp = norm_mlp_g[i].reshape(1, D)
        w1, w2 = mlp_w1[i].astype(BF16), mlp_w2[i].astype(BF16)
        if i == depth - 1:
            npt = PROMPT_ROWS // TM
            y_prompt = _mlp(x, mod, g_mlp, w1, w2, final_g, 0, npt, True, "mlp_final_prompt")
            y_sample = _mlp(x, mod, g_mlp, w1, w2, final_g, npt, N_ROWS // TM - npt, True, "mlp_final_sample")
        else:
            x = _mlp(x, mod, g_mlp, w1, w2, final_g, 0, N_ROWS // TM, False, "mlp")
    return (y_prompt.reshape(N_PROMPT_SEQ, T_PROMPT, D),
            y_sample.reshape(N_SAMPLE_SEQ, T_SAMPLE, D),
            jnp.stack(new_f, axis=1),
            jnp.stack(new_b, axis=1))
```

```python
import functools
import math

import numpy as np
import jax
import jax.numpy as jnp
from jax import lax
from jax.experimental import pallas as pl
from jax.experimental.pallas import tpu as pltpu

F32 = jnp.float32
BF16 = jnp.bfloat16

D = 1024
D_FF = 4 * D
N_PROMPT_SEQ = 16
T_PROMPT = 256
N_SAMPLE_SEQ = 4
T_SAMPLE = 4096
GROUP_ROWS = 4096
N_GROUPS = 1 + N_SAMPLE_SEQ
N_ROWS = N_GROUPS * GROUP_ROWS
PROMPT_ROWS = N_PROMPT_SEQ * T_PROMPT
MOD_ROWS = 16
GRID_W = 64
HEADS = 4
DK = 256
DV = 512
HK = HEADS * DK
HV = HEADS * DV
RET_CHUNK = 256
ROPE_BASE = 10000.0
CONV_WIDTH = 31
CONV_PAD = CONV_WIDTH // 2
CONV_HALO = 16
POOL_WINDOWS = (2, 4, 8, 16)
POOL_HALO = 8
GW = 256
N_CGROUPS = D // GW
NORM_EPS = 1e-6
GN_EPS = 1e-5

TM = 1024
TILES_PER_GROUP = GROUP_ROWS // TM
TS = 256
VMEM_LIMIT = 56 * 1024 * 1024


def _cparams(sem):
    return pltpu.CompilerParams(dimension_semantics=sem, vmem_limit_bytes=VMEM_LIMIT)


def _norm_mod(x, g, shift, scale):
    ms = jnp.mean(x * x, axis=-1, keepdims=True)
    y = (x * lax.rsqrt(ms + NORM_EPS)) * g
    return y * (1.0 + scale) + shift


def _mod_slice(mod_ref, idx):
    return mod_ref[:, idx * D:(idx + 1) * D]


def _silu(x):
    return x * jax.nn.sigmoid(x)


def _mod_kernel(cond_ref, w_ref, b_ref, o_ref):
    a = _silu(cond_ref[...]).astype(BF16)
    o_ref[...] = jnp.dot(a, w_ref[...].astype(BF16), preferred_element_type=F32) + b_ref[...]


def _adaln_all(cond, w_mod, b_mod):
    depth = w_mod.shape[0]
    tn = 1536
    return pl.pallas_call(
        _mod_kernel,
        out_shape=jax.ShapeDtypeStruct((depth, MOD_ROWS, 6 * D), F32),
        grid=(depth, 6 * D // tn),
        in_specs=[
            pl.BlockSpec((MOD_ROWS, D), lambda l, j: (0, 0)),
            pl.BlockSpec((None, D, tn), lambda l, j: (l, 0, j)),
            pl.BlockSpec((None, 1, tn), lambda l, j: (l, 0, j)),
        ],
        out_specs=pl.BlockSpec((None, MOD_ROWS, tn), lambda l, j: (l, 0, j)),
        compiler_params=_cparams(("parallel", "parallel")),
        name="adaln_params",
    )(cond, w_mod, b_mod.reshape(depth, 1, 6 * D))


def _mod_spec():
    return pl.BlockSpec((None, 1, 6 * D), lambda i, j: (i // TILES_PER_GROUP, 0, 0))


def _nm_prologue(x_ref, mod_ref, g_ref, h_scr, shift_idx):
    @pl.when(pl.program_id(1) == 0)
    def _():
        h = _norm_mod(x_ref[...], g_ref[...], _mod_slice(mod_ref, shift_idx), _mod_slice(mod_ref, shift_idx + 1))
        h_scr[...] = h.astype(BF16)


def _nm_glu_kernel(x_ref, mod_ref, g_ref, wa_ref, wg_ref, ba_ref, bg_ref, o_ref, h_scr, *, shift_idx):
    _nm_prologue(x_ref, mod_ref, g_ref, h_scr, shift_idx)
    h = h_scr[...]
    a = jnp.dot(h, wa_ref[...], preferred_element_type=F32) + ba_ref[...]
    gt = jnp.dot(h, wg_ref[...], preferred_element_type=F32) + bg_ref[...]
    o_ref[...] = (a * jax.nn.sigmoid(gt)).astype(o_ref.dtype)


def _nm_call(kernel, x, mod, g, extra_in, extra_specs, n_cols, tn, out_dtype, name):
    return pl.pallas_call(
        kernel,
        out_shape=jax.ShapeDtypeStruct((N_ROWS, n_cols), out_dtype),
        grid=(N_ROWS // TM, n_cols // tn),
        in_specs=[
            pl.BlockSpec((TM, D), lambda i, j: (i, 0)),
            _mod_spec(),
            pl.BlockSpec((1, D), lambda i, j: (0, 0)),
        ] + extra_specs,
        out_specs=pl.BlockSpec((TM, tn), lambda i, j: (i, j)),
        scratch_shapes=[pltpu.VMEM((TM, D), BF16)],
        compiler_params=_cparams(("parallel", "arbitrary")),
        name=name,
    )(x, mod, g, *extra_in)


def _nm_glu(x, mod, g, w, b, shift_idx, tn):
    off = D // tn
    b2 = b.reshape(1, 2 * D)
    return _nm_call(functools.partial(_nm_glu_kernel, shift_idx=shift_idx), x, mod, g,
                    [w, w, b2, b2],
                    [pl.BlockSpec((D, tn), lambda i, j: (0, j)),
                     pl.BlockSpec((D, tn), lambda i, j: (0, j + off)),
                     pl.BlockSpec((1, tn), lambda i, j: (0, j)),
                     pl.BlockSpec((1, tn), lambda i, j: (0, j + off))],
                    D, tn, F32, "conv_pw1_glu")


def _out_kernel(yp_ref, ys_ref, w_ref, b_ref, xp_ref, *rest, gate_idx, n_prompt_tiles):
    xs_ref = rest[0] if len(rest) == 3 else xp_ref
    mod_ref, o_ref = rest[-2:]
    i = pl.program_id(0)
    w = w_ref[...]

    def finish(y_ref, x_ref):
        acc = jnp.dot(y_ref[...], w, preferred_element_type=F32) + b_ref[...]
        o_ref[...] = x_ref[...] + _mod_slice(mod_ref, gate_idx) * acc

    @pl.when(i < n_prompt_tiles)
    def _():
        finish(yp_ref, xp_ref)

    @pl.when(i >= n_prompt_tiles)
    def _():
        finish(ys_ref, xs_ref)


def _out_proj(y_prompt, y_sample, w, b, x, mod, gate_idx, name):
    k = w.shape[0]
    npt = PROMPT_ROWS // TM
    prompt_map = lambda i: (jnp.minimum(i, npt - 1), 0)
    sample_map = lambda i: (jnp.maximum(i - npt, 0), 0)
    if isinstance(x, tuple):
        x_args = list(x)
        x_specs = [pl.BlockSpec((TM, D), prompt_map), pl.BlockSpec((TM, D), sample_map)]
    else:
        x_args = [x]
        x_specs = [pl.BlockSpec((TM, D), lambda i: (i, 0))]
    return pl.pallas_call(
        functools.partial(_out_kernel, gate_idx=gate_idx, n_prompt_tiles=npt),
        out_shape=jax.ShapeDtypeStruct((N_ROWS, D), F32),
        grid=(N_ROWS // TM,),
        in_specs=[
            pl.BlockSpec((TM, k), prompt_map),
            pl.BlockSpec((TM, k), sample_map),
            pl.BlockSpec((k, D), lambda i: (0, 0)),
            pl.BlockSpec((1, D), lambda i: (0, 0)),
        ] + x_specs + [
            pl.BlockSpec((None, 1, 6 * D), lambda i: (i // TILES_PER_GROUP, 0, 0)),
        ],
        out_specs=pl.BlockSpec((TM, D), lambda i: (i, 0)),
        compiler_params=_cparams(("parallel",)),
        name=name,
    )(y_prompt, y_sample, w, b.reshape(1, D), *x_args, mod)


def _mlp_kernel(x_ref, mod_ref, g_ref, w1_ref, w2_ref, fg_ref, o_ref, h_scr, acc_scr, *, n_ff, final_norm):
    f = pl.program_id(1)

    @pl.when(f == 0)
    def _():
        h = _norm_mod(x_ref[...], g_ref[...], _mod_slice(mod_ref, 3), _mod_slice(mod_ref, 4))
        h_scr[...] = h.astype(BF16)
        acc_scr[...] = jnp.zeros_like(acc_scr)

    hid = jnp.maximum(jnp.dot(h_scr[...], w1_ref[...], preferred_element_type=F32), 0.0)
    acc_scr[...] += jnp.dot((hid * hid).astype(BF16), w2_ref[...], preferred_element_type=F32)

    @pl.when(f == n_ff - 1)
    def _():
        y = x_ref[...] + _mod_slice(mod_ref, 5) * acc_scr[...]
        if final_norm:
            ms = jnp.mean(y * y, axis=-1, keepdims=True)
            y = (y * lax.rsqrt(ms + NORM_EPS)) * fg_ref[...]
        o_ref[...] = y


def _mlp(x, mod, g, w1, w2, final_g, row_tile0, n_tiles, final_norm, name):
    tf = 1024
    n_ff = D_FF // tf
    return pl.pallas_call(
        functools.partial(_mlp_kernel, n_ff=n_ff, final_norm=final_norm),
        out_shape=jax.ShapeDtypeStruct((n_tiles * TM, D), F32),
        grid=(n_tiles, n_ff),
        in_specs=[
            pl.BlockSpec((TM, D), lambda i, f: (i + row_tile0, 0)),
            pl.BlockSpec((None, 1, 6 * D), lambda i, f: ((i + row_tile0) // TILES_PER_GROUP, 0, 0)),
            pl.BlockSpec((1, D), lambda i, f: (0, 0)),
            pl.BlockSpec((D, tf), lambda i, f: (0, f)),
            pl.BlockSpec((tf, D), lambda i, f: (f, 0)),
            pl.BlockSpec((1, D), lambda i, f: (0, 0)),
        ],
        out_specs=pl.BlockSpec((TM, D), lambda i, f: (i, 0)),
        scratch_shapes=[pltpu.VMEM((TM, D), BF16), pltpu.VMEM((TM, D), F32)],
        compiler_params=_cparams(("parallel", "arbitrary")),
        name=name,
    )(x, mod, g, w1, w2, final_g)


def _scan_kernel(*refs, chunk, n_chunks, reverse, has_s0, has_zin, emit_state):
    refs = list(refs)
    p_ref, q_ref, k_ref, v_ref, g_ref, gn_ref = refs[:6]
    pos = 6
    s0_ref = zin_ref = sout_ref = None
    if has_s0:
        s0_ref = refs[pos]; pos += 1
    if has_zin:
        zin_ref = refs[pos]; pos += 1
    z_ref = refs[pos]; pos += 1
    if emit_state:
        sout_ref = refs[pos]; pos += 1
    s_scr, d_scr, xi_scr, zeta_scr = refs[pos:pos + 4]

    c = pl.program_id(1)
    log_g = [jnp.log1p(-jnp.exp2(-p_ref[h]))[:, :1] for h in range(HEADS)]

    @pl.when(c == 0)
    def _():
        ri = lax.broadcasted_iota(jnp.int32, (chunk, chunk), 0)
        ci = lax.broadcasted_iota(jnp.int32, (chunk, chunk), 1)
        rel = (ci - ri) if reverse else (ri - ci)
        relf = jnp.maximum(rel, 0).astype(F32)
        t = lax.broadcasted_iota(jnp.int32, (chunk, 128), 0)
        step = ((chunk - 1) - t if reverse else t).astype(F32)
        for h in range(HEADS):
            d_scr[h] = jnp.where(rel >= 0, jnp.exp(log_g[h] * relf), 0.0)
            xi_scr[h] = jnp.exp(log_g[h] * (step + 1.0))
            zeta_scr[h] = jnp.exp(log_g[h] * ((chunk - 1.0) - step))
        if has_s0:
            s_scr[...] = s0_ref[...]
        else:
            s_scr[...] = jnp.zeros_like(s_scr)

    for h in range(HEADS):
        kcols = slice(h * DK, (h + 1) * DK)
        vcols = slice(h * DV, (h + 1) * DV)
        q = q_ref[:, kcols]
        k = k_ref[:, kcols]
        v = v_ref[:, vcols]
        s = s_scr[h]
        scores = lax.dot_general(q, k, (((1,), (1,)), ((), ())), preferred_element_type=F32)
        inner = jnp.dot((scores * d_scr[h]).astype(BF16), v, preferred_element_type=F32)
        cross = jnp.dot(q, s.astype(BF16), preferred_element_type=F32)
        xi = jnp.concatenate([xi_scr[h]] * (DV // 128), axis=-1)
        o = inner + cross * xi
        zeta = jnp.concatenate([zeta_scr[h]] * (DK // 128), axis=-1)
        kz = (k.astype(F32) * zeta).astype(BF16)
        upd = lax.dot_general(kz, v, (((0,), (0,)), ((), ())), preferred_element_type=F32)
        s_scr[h] = s * jnp.exp(log_g[h] * float(chunk)) + upd

        mu = jnp.mean(o, axis=-1, keepdims=True)
        dlt = o - mu
        var = jnp.mean(dlt * dlt, axis=-1, keepdims=True)
        z = _silu(g_ref[:, vcols].astype(F32)) * ((dlt * lax.rsqrt(var + GN_EPS)) * gn_ref[:, vcols])
        if has_zin:
            z = zin_ref[:, vcols] + z
        z_ref[:, vcols] = z.astype(z_ref.dtype)

    if emit_state:
        @pl.when(c == n_chunks - 1)
        def _():
            sout_ref[...] = s_scr[...]


def _scan(decay_p, proj, gn_g, s0, zin, *, row0, n_seq, seq_len, reverse, emit_state, out_dtype, name):
    chunk = RET_CHUNK
    nc = seq_len // chunk
    rb0 = row0 // chunk

    def rb(b, c):
        cc = (nc - 1 - c) if reverse else c
        return b * nc + cc

    in_specs = [
        pl.BlockSpec((HEADS, 1, 128), lambda b, c: (0, 0, 0)),
        pl.BlockSpec((chunk, HK), lambda b, c: (rb0 + rb(b, c), 0)),
        pl.BlockSpec((chunk, HK), lambda b, c: (rb0 + rb(b, c), 1)),
        pl.BlockSpec((chunk, HV), lambda b, c: (rb0 + rb(b, c), 1)),
        pl.BlockSpec((chunk, HV), lambda b, c: (rb0 + rb(b, c), 3 if reverse else 2)),
        pl.BlockSpec((1, HV), lambda b, c: (0, 0)),
    ]
    args = [decay_p, proj, proj, proj, proj, gn_g]
    if s0 is not None:
        s0_arr, s0_layer = s0
        in_specs.append(pl.BlockSpec((None, None, HEADS, DK, DV), lambda b, c: (b, s0_layer, 0, 0, 0)))
        args.append(s0_arr)
    if zin is not None:
        in_specs.append(pl.BlockSpec((chunk, HV), lambda b, c: (rb(b, c), 0)))
        args.append(zin)
    out_shape = [jax.ShapeDtypeStruct((n_seq * seq_len, HV), out_dtype)]
    out_specs = [pl.BlockSpec((chunk, HV), lambda b, c: (rb(b, c), 0))]
    if emit_state:
        out_shape.append(jax.ShapeDtypeStruct((n_seq, HEADS, DK, DV), F32))
        out_specs.append(pl.BlockSpec((None, HEADS, DK, DV), lambda b, c: (b, 0, 0, 0)))
    res = pl.pallas_call(
        functools.partial(_scan_kernel, chunk=chunk, n_chunks=nc, reverse=reverse,
                          has_s0=s0 is not None, has_zin=zin is not None, emit_state=emit_state),
        out_shape=out_shape,
        grid=(n_seq, nc),
        in_specs=in_specs,
        out_specs=out_specs,
        scratch_shapes=[pltpu.VMEM((HEADS, DK, DV), F32), pltpu.VMEM((HEADS, chunk, chunk), F32),
                        pltpu.VMEM((HEADS, chunk, 128), F32), pltpu.VMEM((HEADS, chunk, 128), F32)],
        compiler_params=_cparams(("parallel", "arbitrary")),
        name=name,
    )(*args)
    return res if emit_state else (res[0], None)


def _rope_tables():
    n = DK // 4
    inv = ROPE_BASE ** (-jnp.arange(n, dtype=F32) / n)
    t = jnp.arange(T_SAMPLE, dtype=jnp.int32)
    row = (t // GRID_W).astype(F32)
    col = (t % GRID_W).astype(F32)
    ang = jnp.concatenate([row[:, None] * inv[None, :], col[:, None] * inv[None, :]], axis=-1)
    cos = jnp.concatenate([jnp.ones((PROMPT_ROWS, DK // 2), F32)] + [jnp.cos(ang)] * N_SAMPLE_SEQ, axis=0)
    sin = jnp.concatenate([jnp.zeros((PROMPT_ROWS, DK // 2), F32)] + [jnp.sin(ang)] * N_SAMPLE_SEQ, axis=0)
    return cos, sin


def _ret_proj_kernel(xp_ref, xs_ref, mod_ref, g_ref, w_ref, cos_ref, sin_ref, o_ref, h_scr, *, n_prompt_tiles):
    i = pl.program_id(0)
    j = pl.program_id(1)

    def prologue(x_ref):
        h = _norm_mod(x_ref[...], g_ref[...], _mod_slice(mod_ref, 0), _mod_slice(mod_ref, 1))
        h_scr[...] = h.astype(BF16)

    @pl.when((j == 0) & (i < n_prompt_tiles))
    def _():
        prologue(xp_ref)

    @pl.when((j == 0) & (i >= n_prompt_tiles))
    def _():
        prologue(xs_ref)

    acc = jnp.dot(h_scr[...], w_ref[...], preferred_element_type=F32)

    @pl.when(j < 2)
    def _():
        half = DK // 2
        cos, sin = cos_ref[...], sin_ref[...]
        kscale = jnp.where(j == 1, DK ** -0.5, 1.0).astype(F32)
        for h in range(HEADS):
            x1 = acc[:, h * DK:h * DK + half]
            x2 = acc[:, h * DK + half:(h + 1) * DK]
            o_ref[:, h * DK:h * DK + half] = ((x1 * cos - x2 * sin) * kscale).astype(o_ref.dtype)
            o_ref[:, h * DK + half:(h + 1) * DK] = ((x2 * cos + x1 * sin) * kscale).astype(o_ref.dtype)

    @pl.when(j >= 2)
    def _():
        o_ref[...] = acc.astype(o_ref.dtype)


def _ret_proj(xp, xs, mod, g, w_in, cos, sin):
    npt = PROMPT_ROWS // TM
    half = DK // 2
    n_cols = w_in.shape[1]
    return pl.pallas_call(
        functools.partial(_ret_proj_kernel, n_prompt_tiles=npt),
        out_shape=jax.ShapeDtypeStruct((N_ROWS, n_cols), BF16),
        grid=(N_ROWS // TM, n_cols // HK),
        in_specs=[
            pl.BlockSpec((TM, D), lambda i, j: (jnp.minimum(i, npt - 1), 0)),
            pl.BlockSpec((TM, D), lambda i, j: (jnp.maximum(i - npt, 0), 0)),
            _mod_spec(),
            pl.BlockSpec((1, D), lambda i, j: (0, 0)),
            pl.BlockSpec((D, HK), lambda i, j: (0, j)),
            pl.BlockSpec((TM, half), lambda i, j: (i, 0)),
            pl.BlockSpec((TM, half), lambda i, j: (i, 0)),
        ],
        out_specs=pl.BlockSpec((TM, HK), lambda i, j: (i, j)),
        scratch_shapes=[pltpu.VMEM((TM, D), BF16)],
        compiler_params=_cparams(("parallel", "arbitrary")),
        name="ret_in_proj",
    )(xp, xs, mod, g, w_in, cos, sin)


def _retention_layer(xp, xs, mod, g, w_in, gn_g, w_out, decay_fwd, decay_bwd, s0_fwd, s0_bwd):
    cos, sin = _rope_tables()
    proj = _ret_proj(xp, xs, mod, g, w_in.astype(BF16), cos, sin)
    gn = gn_g.reshape(1, HV)
    pf = jnp.broadcast_to(decay_fwd.astype(F32)[:, None, None], (HEADS, 1, 128))
    pb = jnp.broadcast_to(decay_bwd.astype(F32)[:, None, None], (HEADS, 1, 128))
    prompt = dict(row0=0, n_seq=N_PROMPT_SEQ, seq_len=T_PROMPT, emit_state=True)
    sample = dict(row0=PROMPT_ROWS, n_seq=N_SAMPLE_SEQ, seq_len=T_SAMPLE, emit_state=False)
    zp, sf = _scan(pf, proj, gn, None, None, reverse=False, out_dtype=F32, name="ret_scan_prompt_fwd", **prompt)
    yp, sb = _scan(pb, proj, gn, None, zp, reverse=True, out_dtype=BF16, name="ret_scan_prompt_bwd", **prompt)
    zs, _ = _scan(pf, proj, gn, s0_fwd, None, reverse=False, out_dtype=F32, name="ret_scan_sample_fwd", **sample)
    ys, _ = _scan(pb, proj, gn, s0_bwd, zs, reverse=True, out_dtype=BF16, name="ret_scan_sample_bwd", **sample)
    x = _out_proj(yp, ys, w_out.astype(BF16), jnp.zeros((D,), F32), (xp, xs), mod, 2, "ret_out_proj")
    return x, sf, sb


def _seq_tile_flags(i):
    n_prompt_tiles = PROMPT_ROWS // TS
    tiles_per_seq = T_SAMPLE // TS
    is_sample = i >= n_prompt_tiles
    tin = (i - n_prompt_tiles) % tiles_per_seq
    if T_PROMPT != TS:
        raise NotImplementedError("prompt sequences must be exactly one row tile")
    return is_sample & (tin > 0), is_sample & (tin < tiles_per_seq - 1)


def _conv_kernel(u_ref, up_ref, un_ref, x_ref, mod_ref, wdw_ref, bdw_ref, lng_ref, lnb_ref, w2_ref, b2_ref,
                 o_ref, ext_scr, cv_scr):
    has_prev, has_next = _seq_tile_flags(pl.program_id(0))
    n_slabs = D // 128
    for lt in range(n_slabs):
        lanes = slice(lt * 128, (lt + 1) * 128)
        ext_scr[lt, 0:CONV_HALO, :] = jnp.where(has_prev, up_ref[:, lanes], 0.0)
        ext_scr[lt, CONV_HALO:CONV_HALO + TS, :] = u_ref[:, lanes]
        ext_scr[lt, CONV_HALO + TS:, :] = jnp.where(has_next, un_ref[:, lanes], 0.0)

    rows = 32
    shift = CONV_HALO - CONV_PAD

    def body(lt, carry):
        for r0 in range(0, TS, rows):
            acc = jnp.broadcast_to(bdw_ref[lt], (rows, 128))
            for kk in range(CONV_WIDTH):
                acc = acc + wdw_ref[lt, kk:kk + 1, :] * ext_scr[lt, r0 + kk + shift:r0 + kk + shift + rows, :]
            cv_scr[lt, r0:r0 + rows, :] = acc
        return carry

    lax.fori_loop(0, n_slabs, body, 0)

    cv = jnp.concatenate([cv_scr[lt] for lt in range(n_slabs)], axis=-1)
    mu = jnp.mean(cv, axis=-1, keepdims=True)
    dlt = cv - mu
    var = jnp.mean(dlt * dlt, axis=-1, keepdims=True)
    ln = (dlt * lax.rsqrt(var + GN_EPS)) * lng_ref[...] + lnb_ref[...]
    act = _silu(ln).astype(BF16)
    y = jnp.dot(act, w2_ref[...], preferred_element_type=F32) + b2_ref[...]
    o_ref[...] = x_ref[...] + _mod_slice(mod_ref, 2) * y


def _conv_layer(x, mod, g, w_pw1, b_pw1, w_dw, b_dw, ln_g, ln_b, w_pw2, b_pw2):
    u = _nm_glu(x, mod, g, w_pw1.astype(BF16), b_pw1, 0, D)
    hb = TS // CONV_HALO
    last = N_ROWS // CONV_HALO - 1
    row = lambda a: a.reshape(1, D)
    return pl.pallas_call(
        _conv_kernel,
        out_shape=jax.ShapeDtypeStruct((N_ROWS, D), F32),
        grid=(N_ROWS // TS,),
        in_specs=[
            pl.BlockSpec((TS, D), lambda i: (i, 0)),
            pl.BlockSpec((CONV_HALO, D), lambda i: (jnp.maximum(i * hb - 1, 0), 0)),
            pl.BlockSpec((CONV_HALO, D), lambda i: (jnp.minimum((i + 1) * hb, last), 0)),
            pl.BlockSpec((TS, D), lambda i: (i, 0)),
            pl.BlockSpec((None, 1, 6 * D), lambda i: (i // (GROUP_ROWS // TS), 0, 0)),
            pl.BlockSpec((D // 128, CONV_WIDTH, 128), lambda i: (0, 0, 0)),
            pl.BlockSpec((D // 128, 1, 128), lambda i: (0, 0, 0)),
            pl.BlockSpec((1, D), lambda i: (0, 0)),
            pl.BlockSpec((1, D), lambda i: (0, 0)),
            pl.BlockSpec((D, D), lambda i: (0, 0)),
            pl.BlockSpec((1, D), lambda i: (0, 0)),
        ],
        out_specs=pl.BlockSpec((TS, D), lambda i: (i, 0)),
        scratch_shapes=[pltpu.VMEM((D // 128, TS + 2 * CONV_HALO, 128), F32), pltpu.VMEM((D // 128, TS, 128), F32)],
        compiler_params=_cparams(("parallel",)),
        name="conv_dw_ln_pw2",
    )(u, u, u, x, mod, w_dw.reshape(CONV_WIDTH, D // 128, 128).transpose(1, 0, 2), b_dw.reshape(D // 128, 1, 128),
      row(ln_g), row(ln_b), w_pw2.astype(BF16), row(b_pw2))


def _pool_kernel(x_ref, xp_ref, xn_ref, mod_ref, g_ref, w_ref, sc_ref, o_ref, ext_scr):
    i = pl.program_id(0)
    has_prev, has_next = _seq_tile_flags(i)
    shift, scale = _mod_slice(mod_ref, 0), _mod_slice(mod_ref, 1)
    g = g_ref[...]
    x = x_ref[...]
    h = _norm_mod(x, g, shift, scale)
    ext_scr[0:POOL_HALO, :] = jnp.where(has_prev, _norm_mod(xp_ref[...], g, shift, scale), 0.0)
    ext_scr[POOL_HALO:POOL_HALO + TS, :] = h
    ext_scr[POOL_HALO + TS:, :] = jnp.where(has_next, _norm_mod(xn_ref[...], g, shift, scale), 0.0)

    seq_len = jnp.where(i >= PROMPT_ROWS // TS, T_SAMPLE, T_PROMPT)
    t = (i * TS) % seq_len + lax.broadcasted_iota(jnp.int32, (TS, GW), 0)
    outs = []
    for gi, w in enumerate(POOL_WINDOWS):
        lanes = slice(gi * GW, (gi + 1) * GW)
        tot = ext_scr[pl.ds(POOL_HALO - w // 2, TS), lanes]
        for j in range(1, w):
            tot = tot + ext_scr[pl.ds(POOL_HALO - w // 2 + j, TS), lanes]
        cnt = jnp.minimum(t + w // 2, seq_len) - jnp.maximum(t - w // 2, 0)
        p = tot / cnt.astype(F32) - h[:, lanes]
        outs.append(jnp.dot(p.astype(BF16), w_ref[gi], preferred_element_type=F32))
    y = jnp.concatenate(outs, axis=-1) * sc_ref[...]
    o_ref[...] = x + _mod_slice(mod_ref, 2) * y


def _pool_layer(x, mod, g, w_grp, scale):
    hb = TS // POOL_HALO
    last = N_ROWS // POOL_HALO - 1
    return pl.pallas_call(
        _pool_kernel,
        out_shape=jax.ShapeDtypeStruct((N_ROWS, D), F32),
        grid=(N_ROWS // TS,),
        in_specs=[
            pl.BlockSpec((TS, D), lambda i: (i, 0)),
            pl.BlockSpec((POOL_HALO, D), lambda i: (jnp.maximum(i * hb - 1, 0), 0)),
            pl.BlockSpec((POOL_HALO, D), lambda i: (jnp.minimum((i + 1) * hb, last), 0)),
            pl.BlockSpec((None, 1, 6 * D), lambda i: (i // (GROUP_ROWS // TS), 0, 0)),
            pl.BlockSpec((1, D), lambda i: (0, 0)),
            pl.BlockSpec((N_CGROUPS, GW, GW), lambda i: (0, 0, 0)),
            pl.BlockSpec((1, D), lambda i: (0, 0)),
        ],
        out_specs=pl.BlockSpec((TS, D), lambda i: (i, 0)),
        scratch_shapes=[pltpu.VMEM((TS + 2 * POOL_HALO, D), F32)],
        compiler_params=_cparams(("parallel",)),
        name="pool_mixer",
    )(x, x, x, mod, g, w_grp.astype(BF16), scale.reshape(1, D))


def _chan_dft_kernel(x_ref, mod_ref, g_ref, c_ref, s_ref, a_ref, b_ref):
    h = _norm_mod(x_ref[...], g_ref[...], _mod_slice(mod_ref, 0), _mod_slice(mod_ref, 1)).astype(BF16)
    c, s = c_ref[...], s_ref[...]
    for gi in range(N_CGROUPS):
        lanes = slice(gi * GW, (gi + 1) * GW)
        a_ref[:, lanes] = jnp.dot(h[:, lanes], c, preferred_element_type=F32).astype(a_ref.dtype)
        b_ref[:, lanes] = jnp.dot(h[:, lanes], s, preferred_element_type=F32).astype(b_ref.dtype)


def _dft_tables(n):
    idx = jnp.arange(n, dtype=jnp.int32)
    ang = (2.0 * np.pi / n) * ((idx[:, None] * idx[None, :]) % n).astype(F32)
    return jnp.cos(ang), jnp.sin(ang)


def _chan_dft(x, mod, g):
    c, s = _dft_tables(GW)
    return pl.pallas_call(
        _chan_dft_kernel,
        out_shape=[jax.ShapeDtypeStruct((N_ROWS, D), BF16)] * 2,
        grid=(N_ROWS // TM,),
        in_specs=[
            pl.BlockSpec((TM, D), lambda i: (i, 0)),
            pl.BlockSpec((None, 1, 6 * D), lambda i: (i // TILES_PER_GROUP, 0, 0)),
            pl.BlockSpec((1, D), lambda i: (0, 0)),
            pl.BlockSpec((GW, GW), lambda i: (0, 0)),
            pl.BlockSpec((GW, GW), lambda i: (0, 0)),
        ],
        out_specs=[pl.BlockSpec((TM, D), lambda i: (i, 0))] * 2,
        compiler_params=_cparams(("parallel",)),
        name="fnet_chan_dft",
    )(x, mod, g, c.astype(BF16), (-s).astype(BF16))


def _time_dft_kernel(c_ref, sn_ref, a_ref, b_ref, o_ref, acc_scr, *, n_k, scale):
    k = pl.program_id(2)

    @pl.when(k == 0)
    def _():
        acc_scr[...] = jnp.zeros_like(acc_scr)

    acc_scr[...] += (jnp.dot(c_ref[...], a_ref[...], preferred_element_type=F32)
                     + jnp.dot(sn_ref[...], b_ref[...], preferred_element_type=F32))

    @pl.when(k == n_k - 1)
    def _():
        o_ref[...] = (acc_scr[...] * scale).astype(o_ref.dtype)


def _time_dft(cos_t, sin_t, a, b, *, row0, n_seq, seq_len, name):
    tm = min(seq_len, 1024)
    tk = min(seq_len, 512)
    n_i, n_k = seq_len // tm, seq_len // tk
    kb0 = row0 // tk
    scale = 1.0 / math.sqrt(seq_len * GW)
    return pl.pallas_call(
        functools.partial(_time_dft_kernel, n_k=n_k, scale=scale),
        out_shape=jax.ShapeDtypeStruct((n_seq * seq_len, D), BF16),
        grid=(n_seq, n_i, n_k),
        in_specs=[
            pl.BlockSpec((tm, tk), lambda s, i, k: (i, k)),
            pl.BlockSpec((tm, tk), lambda s, i, k: (i, k)),
            pl.BlockSpec((tk, D), lambda s, i, k: (kb0 + s * n_k + k, 0)),
            pl.BlockSpec((tk, D), lambda s, i, k: (kb0 + s * n_k + k, 0)),
        ],
        out_specs=pl.BlockSpec((tm, D), lambda s, i, k: (s * n_i + i, 0)),
        scratch_shapes=[pltpu.VMEM((tm, D), F32)],
        compiler_params=_cparams(("parallel", "parallel", "arbitrary")),
        name=name,
    )(cos_t, sin_t, a, b)


def _cmul_const(xr, xi, wr, wi):
    def scaled(v, s):
        if s == 0.0:
            return None
        return v if s == 1.0 else (-v if s == -1.0 else v * s)

    def add(p, q):
        if p is None:
            return q
        return p if q is None else p + q

    return add(scaled(xr, wr), scaled(xi, -wi)), add(scaled(xi, wr), scaled(xr, wi))


def _fft_slabs(xr, xi):
    n = len(xr)
    if n == 1:
        return xr, xi
    er, ei = _fft_slabs(xr[0::2], xi[0::2])
    dr, di = _fft_slabs(xr[1::2], xi[1::2])
    out_r, out_i = [None] * n, [None] * n
    for k in range(n // 2):
        wr = float(round(math.cos(2.0 * math.pi * k / n), 15))
        wi = float(round(-math.sin(2.0 * math.pi * k / n), 15))
        tr, ti = _cmul_const(dr[k], di[k], wr, wi)
        out_r[k], out_i[k] = er[k] + tr, ei[k] + ti
        out_r[k + n // 2], out_i[k + n // 2] = er[k] - tr, ei[k] - ti
    return out_r, out_i


FFT_N1 = 8
FFT_N2 = T_SAMPLE // FFT_N1
FFT_LANES = 256
FFT_ROWS = 16


def _time_fft_kernel(a_ref, b_ref, twc_ref, tws_ref, m_ref, o_ref, z_scr, o_scr, *, scale):
    n_slabs = FFT_LANES // 128
    for lt in range(n_slabs):
        lanes = slice(lt * 128, (lt + 1) * 128)

        def body(j, carry, lanes=lanes):
            r0 = pl.multiple_of(j * FFT_ROWS, FFT_ROWS)
            xr = [a_ref[pl.ds(s * FFT_N2 + r0, FFT_ROWS), lanes].astype(F32) for s in range(FFT_N1)]
            xi = [b_ref[pl.ds(s * FFT_N2 + r0, FFT_ROWS), lanes].astype(F32) for s in range(FFT_N1)]
            yr, yi = _fft_slabs(xr, xi)
            for c in range(FFT_N1):
                if c == 0:
                    zr, zi = yr[c], yi[c]
                else:
                    tc = twc_ref[c, pl.ds(r0, FFT_ROWS), :]
                    ts = tws_ref[c, pl.ds(r0, FFT_ROWS), :]
                    zr = yr[c] * tc + yi[c] * ts
                    zi = yi[c] * tc - yr[c] * ts
                z_scr[c, pl.ds(r0, FFT_ROWS), lanes] = zr.astype(BF16)
                z_scr[c, pl.ds(FFT_N2 + r0, FFT_ROWS), lanes] = zi.astype(BF16)
            return carry

        lax.fori_loop(0, FFT_N2 // FFT_ROWS, body, 0)

    m = m_ref[...]
    for c in range(FFT_N1):
        r = jnp.dot(m, z_scr[c], preferred_element_type=F32) * scale
        for lt in range(n_slabs):
            o_scr[lt, pl.ds(c, FFT_N2, stride=FFT_N1), :] = r[:, lt * 128:(lt + 1) * 128]
    for lt in range(n_slabs):
        o_ref[:, lt * 128:(lt + 1) * 128] = o_scr[lt].astype(o_ref.dtype)


def _time_fft_sample(a, b):
    bidx = jnp.arange(FFT_N2, dtype=jnp.int32)
    cidx = jnp.arange(FFT_N1, dtype=jnp.int32)
    ang_tw = (2.0 * np.pi / T_SAMPLE) * (cidx[:, None] * bidx[None, :]).astype(F32)
    twc = jnp.broadcast_to(jnp.cos(ang_tw)[:, :, None], (FFT_N1, FFT_N2, 128))
    tws = jnp.broadcast_to(jnp.sin(ang_tw)[:, :, None], (FFT_N1, FFT_N2, 128))
    c2, s2 = _dft_tables(FFT_N2)
    m = jnp.concatenate([c2, s2], axis=1).astype(BF16)
    rb0 = PROMPT_ROWS // T_SAMPLE
    return pl.pallas_call(
        functools.partial(_time_fft_kernel, scale=1.0 / math.sqrt(T_SAMPLE * GW)),
        out_shape=jax.ShapeDtypeStruct((N_SAMPLE_SEQ * T_SAMPLE, D), BF16),
        grid=(N_SAMPLE_SEQ, D // FFT_LANES),
        in_specs=[
            pl.BlockSpec((T_SAMPLE, FFT_LANES), lambda s, l: (rb0 + s, l)),
            pl.BlockSpec((T_SAMPLE, FFT_LANES), lambda s, l: (rb0 + s, l)),
            pl.BlockSpec((FFT_N1, FFT_N2, 128), lambda s, l: (0, 0, 0)),
            pl.BlockSpec((FFT_N1, FFT_N2, 128), lambda s, l: (0, 0, 0)),
            pl.BlockSpec((FFT_N2, 2 * FFT_N2), lambda s, l: (0, 0)),
        ],
        out_specs=pl.BlockSpec((T_SAMPLE, FFT_LANES), lambda s, l: (s, l)),
        scratch_shapes=[pltpu.VMEM((FFT_N1, 2 * FFT_N2, FFT_LANES), BF16),
                        pltpu.VMEM((FFT_LANES // 128, T_SAMPLE, 128), F32)],
        compiler_params=_cparams(("parallel", "parallel")),
        name="fnet_time_fft_sample",
    )(a, b, twc, tws, m)


def _fourier_layer(x, mod, g, w, b):
    a, bn = _chan_dft(x, mod, g)
    cp, sp = _dft_tables(T_PROMPT)
    fp = _time_dft(cp.astype(BF16), sp.astype(BF16), a, bn,
                   row0=0, n_seq=N_PROMPT_SEQ, seq_len=T_PROMPT, name="fnet_time_dft_prompt")
    fs = _time_fft_sample(a, bn)
    return _out_proj(fp, fs, w.astype(BF16), b, x, mod, 2, "fnet_out_proj")


def kernel(x_prompt, x_sample, state_ret_fwd, state_ret_bwd, c, c_ctx, w_mod, b_mod, norm_mix_g, norm_mlp_g, mlp_w1, mlp_w2, ret_w_in, ret_gn_g, ret_w_out, ret_decay_fwd, ret_decay_bwd, conv_w_pw1, conv_b_pw1, conv_w_dw, conv_b_dw, conv_ln_g, conv_ln_b, conv_w_pw2, conv_b_pw2, pool_w, pool_scale, fnet_w, fnet_b, final_norm_g):
    depth = w_mod.shape[0]
    x = (x_prompt.reshape(PROMPT_ROWS, D), x_sample.reshape(N_SAMPLE_SEQ * T_SAMPLE, D))
    cond = jnp.concatenate([c_ctx[None, :], c, jnp.zeros((MOD_ROWS - N_GROUPS, D), F32)], axis=0)
    mod_all = _adaln_all(cond, w_mod, b_mod).reshape(depth, MOD_ROWS, 1, 6 * D)
    final_g = final_norm_g.reshape(1, D)
    new_f, new_b = [], []
    y_prompt = y_sample = None
    for i in range(depth):
        kind, j = i % 4, i // 4
        mod = mod_all[i]
        g_mix = norm_mix_g[i].reshape(1, D)
        if kind != 0 and isinstance(x, tuple):
            x = jnp.concatenate(x, axis=0)
        if kind == 0:
            xp, xs = x if isinstance(x, tuple) else (x[:PROMPT_ROWS], x[PROMPT_ROWS:])
            x, sf, sb = _retention_layer(xp, xs, mod, g_mix, ret_w_in[j], ret_gn_g[j], ret_w_out[j],
                                         ret_decay_fwd[j], ret_decay_bwd[j],
                                         (state_ret_fwd, j), (state_ret_bwd, j))
            new_f.append(sf)
            new_b.append(sb)
        elif kind == 1:
            x = _conv_layer(x, mod, g_mix, conv_w_pw1[j], conv_b_pw1[j], conv_w_dw[j], conv_b_dw[j],
                            conv_ln_g[j], conv_ln_b[j], conv_w_pw2[j], conv_b_pw2[j])
        elif kind == 2:
            x = _pool_layer(x, mod, g_mix, pool_w[j], pool_scale[j])
        else:
            x = _fourier_layer(x, mod, g_mix, fnet_w[j], fnet_b[j])
        g_mlp = norm_mlp_g[i].reshape(1, D)
        w1, w2 = mlp_w1[i].astype(BF16), mlp_w2[i].astype(BF16)
        if i == depth - 1:
            npt = PROMPT_ROWS // TM
            y_prompt = _mlp(x, mod, g_mlp, w1, w2, final_g, 0, npt, True, "mlp_final_prompt")
            y_sample = _mlp(x, mod, g_mlp, w1, w2, final_g, npt, N_ROWS // TM - npt, True, "mlp_final_sample")
        else:
            x = _mlp(x, mod, g_mlp, w1, w2, final_g, 0, N_ROWS // TM, False, "mlp")
    return (y_prompt.reshape(N_PROMPT_SEQ, T_PROMPT, D),
            y_sample.reshape(N_SAMPLE_SEQ, T_SAMPLE, D),
            jnp.stack(new_f, axis=1),
            jnp.stack(new_b, axis=1))
```

```python
import functools
import math

import numpy as np
import jax
import jax.numpy as jnp
from jax import lax
from jax.experimental import pallas as pl
from jax.experimental.pallas import tpu as pltpu

F32 = jnp.float32
BF16 = jnp.bfloat16

D = 1024
D_FF = 4 * D
N_PROMPT_SEQ = 16
T_PROMPT = 256
N_SAMPLE_SEQ = 4
T_SAMPLE = 4096
GROUP_ROWS = 4096
N_GROUPS = 1 + N_SAMPLE_SEQ
N_ROWS = N_GROUPS * GROUP_ROWS
PROMPT_ROWS = N_PROMPT_SEQ * T_PROMPT
MOD_ROWS = 16
GRID_W = 64
HEADS = 4
DK = 256
DV = 512
HK = HEADS * DK
HV = HEADS * DV
RET_CHUNK = 256
ROPE_BASE = 10000.0
CONV_WIDTH = 31
CONV_PAD = CONV_WIDTH // 2
CONV_HALO = 16
POOL_WINDOWS = (2, 4, 8, 16)
POOL_HALO = 8
GW = 256
N_CGROUPS = D // GW
NORM_EPS = 1e-6
GN_EPS = 1e-5

TM = 1024
TILES_PER_GROUP = GROUP_ROWS // TM
TS = 256
VMEM_LIMIT = 56 * 1024 * 1024


def _cparams(sem):
    return pltpu.CompilerParams(dimension_semantics=sem, vmem_limit_bytes=VMEM_LIMIT)


def _norm_mod(x, g, shift, scale):
    ms = jnp.mean(x * x, axis=-1, keepdims=True)
    y = (x * lax.rsqrt(ms + NORM_EPS)) * g
    return y * (1.0 + scale) + shift


def _mod_slice(mod_ref, idx):
    return mod_ref[:, idx * D:(idx + 1) * D]


def _silu(x):
    return x * jax.nn.sigmoid(x)


def _mod_kernel(cond_ref, w_ref, b_ref, o_ref):
    a = _silu(cond_ref[...]).astype(BF16)
    o_ref[...] = jnp.dot(a, w_ref[...].astype(BF16), preferred_element_type=F32) + b_ref[...]


def _adaln_all(cond, w_mod, b_mod):
    depth = w_mod.shape[0]
    tn = 1536
    return pl.pallas_call(
        _mod_kernel,
        out_shape=jax.ShapeDtypeStruct((depth, MOD_ROWS, 6 * D), F32),
        grid=(depth, 6 * D // tn),
        in_specs=[
            pl.BlockSpec((MOD_ROWS, D), lambda l, j: (0, 0)),
            pl.BlockSpec((None, D, tn), lambda l, j: (l, 0, j)),
            pl.BlockSpec((None, 1, tn), lambda l, j: (l, 0, j)),
        ],
        out_specs=pl.BlockSpec((None, MOD_ROWS, tn), lambda l, j: (l, 0, j)),
        compiler_params=_cparams(("parallel", "parallel")),
        name="adaln_params",
    )(cond, w_mod, b_mod.reshape(depth, 1, 6 * D))


def _mod_spec():
    return pl.BlockSpec((None, 1, 6 * D), lambda i, j: (i // TILES_PER_GROUP, 0, 0))


def _nm_prologue(x_ref, mod_ref, g_ref, h_scr, shift_idx):
    @pl.when(pl.program_id(1) == 0)
    def _():
        h = _norm_mod(x_ref[...], g_ref[...], _mod_slice(mod_ref, shift_idx), _mod_slice(mod_ref, shift_idx + 1))
        h_scr[...] = h.astype(BF16)


def _nm_glu_kernel(x_ref, mod_ref, g_ref, wa_ref, wg_ref, ba_ref, bg_ref, o_ref, h_scr, *, shift_idx):
    _nm_prologue(x_ref, mod_ref, g_ref, h_scr, shift_idx)
    h = h_scr[...]
    a = jnp.dot(h, wa_ref[...], preferred_element_type=F32) + ba_ref[...]
    gt = jnp.dot(h, wg_ref[...], preferred_element_type=F32) + bg_ref[...]
    o_ref[...] = (a * jax.nn.sigmoid(gt)).astype(o_ref.dtype)


def _nm_call(kernel, x, mod, g, extra_in, extra_specs, n_cols, tn, out_dtype, name):
    return pl.pallas_call(
        kernel,
        out_shape=jax.ShapeDtypeStruct((N_ROWS, n_cols), out_dtype),
        grid=(N_ROWS // TM, n_cols // tn),
        in_specs=[
            pl.BlockSpec((TM, D), lambda i, j: (i, 0)),
            _mod_spec(),
            pl.BlockSpec((1, D), lambda i, j: (0, 0)),
        ] + extra_specs,
        out_specs=pl.BlockSpec((TM, tn), lambda i, j: (i, j)),
        scratch_shapes=[pltpu.VMEM((TM, D), BF16)],
        compiler_params=_cparams(("parallel", "arbitrary")),
        name=name,
    )(x, mod, g, *extra_in)


def _nm_glu(x, mod, g, w, b, shift_idx, tn):
    off = D // tn
    b2 = b.reshape(1, 2 * D)
    return _nm_call(functools.partial(_nm_glu_kernel, shift_idx=shift_idx), x, mod, g,
                    [w, w, b2, b2],
                    [pl.BlockSpec((D, tn), lambda i, j: (0, j)),
                     pl.BlockSpec((D, tn), lambda i, j: (0, j + off)),
                     pl.BlockSpec((1, tn), lambda i, j: (0, j)),
                     pl.BlockSpec((1, tn), lambda i, j: (0, j + off))],
                    D, tn, F32, "conv_pw1_glu")


def _out_kernel(yp_ref, ys_ref, w_ref, b_ref, xp_ref, *rest, gate_idx, n_prompt_tiles):
    xs_ref = rest[0] if len(rest) == 3 else xp_ref
    mod_ref, o_ref = rest[-2:]
    i = pl.program_id(0)
    w = w_ref[...]

    def finish(y_ref, x_ref):
        acc = jnp.dot(y_ref[...], w, preferred_element_type=F32) + b_ref[...]
        o_ref[...] = x_ref[...] + _mod_slice(mod_ref, gate_idx) * acc

    @pl.when(i < n_prompt_tiles)
    def _():
        finish(yp_ref, xp_ref)

    @pl.when(i >= n_prompt_tiles)
    def _():
        finish(ys_ref, xs_ref)


def _out_proj(y_prompt, y_sample, w, b, x, mod, gate_idx, name):
    k = w.shape[0]
    npt = PROMPT_ROWS // TM
    prompt_map = lambda i: (jnp.minimum(i, npt - 1), 0)
    sample_map = lambda i: (jnp.maximum(i - npt, 0), 0)
    if isinstance(x, tuple):
        x_args = list(x)
        x_specs = [pl.BlockSpec((TM, D), prompt_map), pl.BlockSpec((TM, D), sample_map)]
    else:
        x_args = [x]
        x_specs = [pl.BlockSpec((TM, D), lambda i: (i, 0))]
    return pl.pallas_call(
        functools.partial(_out_kernel, gate_idx=gate_idx, n_prompt_tiles=npt),
        out_shape=jax.ShapeDtypeStruct((N_ROWS, D), F32),
        grid=(N_ROWS // TM,),
        in_specs=[
            pl.BlockSpec((TM, k), prompt_map),
            pl.BlockSpec((TM, k), sample_map),
            pl.BlockSpec((k, D), lambda i: (0, 0)),
            pl.BlockSpec((1, D), lambda i: (0, 0)),
        ] + x_specs + [
            pl.BlockSpec((None, 1, 6 * D), lambda i: (i // TILES_PER_GROUP, 0, 0)),
        ],
        out_specs=pl.BlockSpec((TM, D), lambda i: (i, 0)),
        compiler_params=_cparams(("parallel",)),
        name=name,
    )(y_prompt, y_sample, w, b.reshape(1, D), *x_args, mod)


def _mlp_kernel(x_ref, mod_ref, g_ref, w1_ref, w2_ref, fg_ref, o_ref, *, tf, final_norm):
    x = x_ref[...]
    h = _norm_mod(x, g_ref[...], _mod_slice(mod_ref, 3), _mod_slice(mod_ref, 4)).astype(BF16)
    acc = None
    for f0 in range(0, D_FF, tf):
        hid = jnp.maximum(jnp.dot(h, w1_ref[:, f0:f0 + tf], preferred_element_type=F32), 0.0)
        part = jnp.dot((hid * hid).astype(BF16), w2_ref[f0:f0 + tf, :], preferred_element_type=F32)
        acc = part if acc is None else acc + part
    y = x + _mod_slice(mod_ref, 5) * acc
    if final_norm:
        ms = jnp.mean(y * y, axis=-1, keepdims=True)
        y = (y * lax.rsqrt(ms + NORM_EPS)) * fg_ref[...]
    o_ref[...] = y


def _resident_spec(shape):
    return pl.BlockSpec(shape, lambda *_: (0,) * len(shape), pipeline_mode=pl.Buffered(1))


def _mlp(x, mod, g, w1, w2, final_g, row_tile0, n_tiles, final_norm, name):
    return pl.pallas_call(
        functools.partial(_mlp_kernel, tf=1024, final_norm=final_norm),
        out_shape=jax.ShapeDtypeStruct((n_tiles * TM, D), F32),
        grid=(n_tiles,),
        in_specs=[
            pl.BlockSpec((TM, D), lambda i: (i + row_tile0, 0)),
            pl.BlockSpec((None, 1, 6 * D), lambda i: ((i + row_tile0) // TILES_PER_GROUP, 0, 0)),
            pl.BlockSpec((1, D), lambda i: (0, 0)),
            _resident_spec((D, D_FF)),
            _resident_spec((D_FF, D)),
            pl.BlockSpec((1, D), lambda i: (0, 0)),
        ],
        out_specs=pl.BlockSpec((TM, D), lambda i: (i, 0)),
        compiler_params=_cparams(("parallel",)),
        name=name,
    )(x, mod, g, w1, w2, final_g)


def _scan_kernel(*refs, chunk, n_chunks, reverse, has_s0, has_zin, emit_state):
    refs = list(refs)
    p_ref, q_ref, k_ref, v_ref, g_ref, gn_ref = refs[:6]
    pos = 6
    s0_ref = zin_ref = sout_ref = None
    if has_s0:
        s0_ref = refs[pos]; pos += 1
    if has_zin:
        zin_ref = refs[pos]; pos += 1
    z_ref = refs[pos]; pos += 1
    if emit_state:
        sout_ref = refs[pos]; pos += 1
    s_scr, d_scr, xi_scr, zeta_scr = refs[pos:pos + 4]

    c = pl.program_id(1)
    log_g = [jnp.log1p(-jnp.exp2(-p_ref[h]))[:, :1] for h in range(HEADS)]

    @pl.when(c == 0)
    def _():
        ri = lax.broadcasted_iota(jnp.int32, (chunk, chunk), 0)
        ci = lax.broadcasted_iota(jnp.int32, (chunk, chunk), 1)
        rel = (ci - ri) if reverse else (ri - ci)
        relf = jnp.maximum(rel, 0).astype(F32)
        t = lax.broadcasted_iota(jnp.int32, (chunk, 128), 0)
        step = ((chunk - 1) - t if reverse else t).astype(F32)
        for h in range(HEADS):
            d_scr[h] = jnp.where(rel >= 0, jnp.exp(log_g[h] * relf), 0.0)
            xi_scr[h] = jnp.exp(log_g[h] * (step + 1.0))
            zeta_scr[h] = jnp.exp(log_g[h] * ((chunk - 1.0) - step))
        if has_s0:
            s_scr[...] = s0_ref[...]
        else:
            s_scr[...] = jnp.zeros_like(s_scr)

    for h in range(HEADS):
        kcols = slice(h * DK, (h + 1) * DK)
        vcols = slice(h * DV, (h + 1) * DV)
        q = q_ref[:, kcols]
        k = k_ref[:, kcols]
        v = v_ref[:, vcols]
        s = s_scr[h]
        scores = lax.dot_general(q, k, (((1,), (1,)), ((), ())), preferred_element_type=F32)
        inner = jnp.dot((scores * d_scr[h]).astype(BF16), v, preferred_element_type=F32)
        cross = jnp.dot(q, s.astype(BF16), preferred_element_type=F32)
        xi = jnp.concatenate([xi_scr[h]] * (DV // 128), axis=-1)
        o = inner + cross * xi
        zeta = jnp.concatenate([zeta_scr[h]] * (DK // 128), axis=-1)
        kz = (k.astype(F32) * zeta).astype(BF16)
        upd = lax.dot_general(kz, v, (((0,), (0,)), ((), ())), preferred_element_type=F32)
        s_scr[h] = s * jnp.exp(log_g[h] * float(chunk)) + upd

        mu = jnp.mean(o, axis=-1, keepdims=True)
        dlt = o - mu
        var = jnp.mean(dlt * dlt, axis=-1, keepdims=True)
        z = _silu(g_ref[:, vcols].astype(F32)) * ((dlt * lax.rsqrt(var + GN_EPS)) * gn_ref[:, vcols])
        if has_zin:
            z = zin_ref[:, vcols] + z
        z_ref[:, vcols] = z.astype(z_ref.dtype)

    if emit_state:
        @pl.when(c == n_chunks - 1)
        def _():
            sout_ref[...] = s_scr[...]


def _scan(decay_p, proj, gn_g, s0, zin, *, row0, n_seq, seq_len, reverse, emit_state, out_dtype, name):
    chunk = RET_CHUNK
    nc = seq_len // chunk
    rb0 = row0 // chunk

    def rb(b, c):
        cc = (nc - 1 - c) if reverse else c
        return b * nc + cc

    in_specs = [
        pl.BlockSpec((HEADS, 1, 128), lambda b, c: (0, 0, 0)),
        pl.BlockSpec((chunk, HK), lambda b, c: (rb0 + rb(b, c), 0)),
        pl.BlockSpec((chunk, HK), lambda b, c: (rb0 + rb(b, c), 1)),
        pl.BlockSpec((chunk, HV), lambda b, c: (rb0 + rb(b, c), 1)),
        pl.BlockSpec((chunk, HV), lambda b, c: (rb0 + rb(b, c), 3 if reverse else 2)),
        pl.BlockSpec((1, HV), lambda b, c: (0, 0)),
    ]
    args = [decay_p, proj, proj, proj, proj, gn_g]
    if s0 is not None:
        s0_arr, s0_layer = s0
        in_specs.append(pl.BlockSpec((None, None, HEADS, DK, DV), lambda b, c: (b, s0_layer, 0, 0, 0)))
        args.append(s0_arr)
    if zin is not None:
        in_specs.append(pl.BlockSpec((chunk, HV), lambda b, c: (rb(b, c), 0)))
        args.append(zin)
    out_shape = [jax.ShapeDtypeStruct((n_seq * seq_len, HV), out_dtype)]
    out_specs = [pl.BlockSpec((chunk, HV), lambda b, c: (rb(b, c), 0))]
    if emit_state:
        out_shape.append(jax.ShapeDtypeStruct((n_seq, HEADS, DK, DV), F32))
        out_specs.append(pl.BlockSpec((None, HEADS, DK, DV), lambda b, c: (b, 0, 0, 0)))
    res = pl.pallas_call(
        functools.partial(_scan_kernel, chunk=chunk, n_chunks=nc, reverse=reverse,
                          has_s0=s0 is not None, has_zin=zin is not None, emit_state=emit_state),
        out_shape=out_shape,
        grid=(n_seq, nc),
        in_specs=in_specs,
        out_specs=out_specs,
        scratch_shapes=[pltpu.VMEM((HEADS, DK, DV), F32), pltpu.VMEM((HEADS, chunk, chunk), F32),
                        pltpu.VMEM((HEADS, chunk, 128), F32), pltpu.VMEM((HEADS, chunk, 128), F32)],
        compiler_params=_cparams(("parallel", "arbitrary")),
        name=name,
    )(*args)
    return res if emit_state else (res[0], None)


def _rope_tables():
    n = DK // 4
    inv = ROPE_BASE ** (-jnp.arange(n, dtype=F32) / n)
    t = jnp.arange(T_SAMPLE, dtype=jnp.int32)
    row = (t // GRID_W).astype(F32)
    col = (t % GRID_W).astype(F32)
    ang = jnp.concatenate([row[:, None] * inv[None, :], col[:, None] * inv[None, :]], axis=-1)
    cos = jnp.concatenate([jnp.ones((PROMPT_ROWS, DK // 2), F32)] + [jnp.cos(ang)] * N_SAMPLE_SEQ, axis=0)
    sin = jnp.concatenate([jnp.zeros((PROMPT_ROWS, DK // 2), F32)] + [jnp.sin(ang)] * N_SAMPLE_SEQ, axis=0)
    return cos, sin


def _ret_proj_kernel(xp_ref, xs_ref, mod_ref, g_ref, w_ref, cos_ref, sin_ref, o_ref, *, n_prompt_tiles):
    x = jnp.where(pl.program_id(0) < n_prompt_tiles, xp_ref[...], xs_ref[...])
    h = _norm_mod(x, g_ref[...], _mod_slice(mod_ref, 0), _mod_slice(mod_ref, 1)).astype(BF16)
    half = DK // 2
    cos, sin = cos_ref[...], sin_ref[...]
    for j in range(w_ref.shape[1] // HK):
        acc = jnp.dot(h, w_ref[:, j * HK:(j + 1) * HK], preferred_element_type=F32)
        if j < 2:
            kscale = DK ** -0.5 if j == 1 else 1.0
            for hd in range(HEADS):
                c0 = j * HK + hd * DK
                x1 = acc[:, hd * DK:hd * DK + half]
                x2 = acc[:, hd * DK + half:(hd + 1) * DK]
                o_ref[:, c0:c0 + half] = ((x1 * cos - x2 * sin) * kscale).astype(o_ref.dtype)
                o_ref[:, c0 + half:c0 + DK] = ((x2 * cos + x1 * sin) * kscale).astype(o_ref.dtype)
        else:
            o_ref[:, j * HK:(j + 1) * HK] = acc.astype(o_ref.dtype)


def _ret_proj(xp, xs, mod, g, w_in, cos, sin):
    tm = TM // 2
    npt = PROMPT_ROWS // tm
    half = DK // 2
    n_cols = w_in.shape[1]
    return pl.pallas_call(
        functools.partial(_ret_proj_kernel, n_prompt_tiles=npt),
        out_shape=jax.ShapeDtypeStruct((N_ROWS, n_cols), BF16),
        grid=(N_ROWS // tm,),
        in_specs=[
            pl.BlockSpec((tm, D), lambda i: (jnp.minimum(i, npt - 1), 0)),
            pl.BlockSpec((tm, D), lambda i: (jnp.maximum(i - npt, 0), 0)),
            pl.BlockSpec((None, 1, 6 * D), lambda i: (i // (GROUP_ROWS // tm), 0, 0)),
            pl.BlockSpec((1, D), lambda i: (0, 0)),
            _resident_spec((D, n_cols)),
            pl.BlockSpec((tm, half), lambda i: (i, 0)),
            pl.BlockSpec((tm, half), lambda i: (i, 0)),
        ],
        out_specs=pl.BlockSpec((tm, n_cols), lambda i: (i, 0)),
        compiler_params=_cparams(("parallel",)),
        name="ret_in_proj",
    )(xp, xs, mod, g, w_in, cos, sin)


def _retention_layer(xp, xs, mod, g, w_in, gn_g, w_out, decay_fwd, decay_bwd, s0_fwd, s0_bwd):
    cos, sin = _rope_tables()
    proj = _ret_proj(xp, xs, mod, g, w_in.astype(BF16), cos, sin)
    gn = gn_g.reshape(1, HV)
    pf = jnp.broadcast_to(decay_fwd.astype(F32)[:, None, None], (HEADS, 1, 128))
    pb = jnp.broadcast_to(decay_bwd.astype(F32)[:, None, None], (HEADS, 1, 128))
    prompt = dict(row0=0, n_seq=N_PROMPT_SEQ, seq_len=T_PROMPT, emit_state=True)
    sample = dict(row0=PROMPT_ROWS, n_seq=N_SAMPLE_SEQ, seq_len=T_SAMPLE, emit_state=False)
    zp, sf = _scan(pf, proj, gn, None, None, reverse=False, out_dtype=F32, name="ret_scan_prompt_fwd", **prompt)
    yp, sb = _scan(pb, proj, gn, None, zp, reverse=True, out_dtype=BF16, name="ret_scan_prompt_bwd", **prompt)
    zs, _ = _scan(pf, proj, gn, s0_fwd, None, reverse=False, out_dtype=F32, name="ret_scan_sample_fwd", **sample)
    ys, _ = _scan(pb, proj, gn, s0_bwd, zs, reverse=True, out_dtype=BF16, name="ret_scan_sample_bwd", **sample)
    x = _out_proj(yp, ys, w_out.astype(BF16), jnp.zeros((D,), F32), (xp, xs), mod, 2, "ret_out_proj")
    return x, sf, sb


def _seq_tile_flags(i):
    n_prompt_tiles = PROMPT_ROWS // TS
    tiles_per_seq = T_SAMPLE // TS
    is_sample = i >= n_prompt_tiles
    tin = (i - n_prompt_tiles) % tiles_per_seq
    if T_PROMPT != TS:
        raise NotImplementedError("prompt sequences must be exactly one row tile")
    return is_sample & (tin > 0), is_sample & (tin < tiles_per_seq - 1)


def _conv_kernel(u_ref, up_ref, un_ref, x_ref, mod_ref, wdw_ref, bdw_ref, lng_ref, lnb_ref, w2_ref, b2_ref,
                 o_ref, ext_scr, cv_scr):
    has_prev, has_next = _seq_tile_flags(pl.program_id(0))
    n_slabs = D // 128
    for lt in range(n_slabs):
        lanes = slice(lt * 128, (lt + 1) * 128)
        ext_scr[lt, 0:CONV_HALO, :] = jnp.where(has_prev, up_ref[:, lanes], 0.0)
        ext_scr[lt, CONV_HALO:CONV_HALO + TS, :] = u_ref[:, lanes]
        ext_scr[lt, CONV_HALO + TS:, :] = jnp.where(has_next, un_ref[:, lanes], 0.0)

    rows = 32
    shift = CONV_HALO - CONV_PAD

    def body(lt, carry):
        for r0 in range(0, TS, rows):
            acc = jnp.broadcast_to(bdw_ref[lt], (rows, 128))
            for kk in range(CONV_WIDTH):
                acc = acc + wdw_ref[lt, kk:kk + 1, :] * ext_scr[lt, r0 + kk + shift:r0 + kk + shift + rows, :]
            cv_scr[lt, r0:r0 + rows, :] = acc
        return carry

    lax.fori_loop(0, n_slabs, body, 0)

    cv = jnp.concatenate([cv_scr[lt] for lt in range(n_slabs)], axis=-1)
    mu = jnp.mean(cv, axis=-1, keepdims=True)
    dlt = cv - mu
    var = jnp.mean(dlt * dlt, axis=-1, keepdims=True)
    ln = (dlt * lax.rsqrt(var + GN_EPS)) * lng_ref[...] + lnb_ref[...]
    act = _silu(ln).astype(BF16)
    y = jnp.dot(act, w2_ref[...], preferred_element_type=F32) + b2_ref[...]
    o_ref[...] = x_ref[...] + _mod_slice(mod_ref, 2) * y


def _conv_layer(x, mod, g, w_pw1, b_pw1, w_dw, b_dw, ln_g, ln_b, w_pw2, b_pw2):
    u = _nm_glu(x, mod, g, w_pw1.astype(BF16), b_pw1, 0, D)
    hb = TS // CONV_HALO
    last = N_ROWS // CONV_HALO - 1
    row = lambda a: a.reshape(1, D)
    return pl.pallas_call(
        _conv_kernel,
        out_shape=jax.ShapeDtypeStruct((N_ROWS, D), F32),
        grid=(N_ROWS // TS,),
        in_specs=[
            pl.BlockSpec((TS, D), lambda i: (i, 0)),
            pl.BlockSpec((CONV_HALO, D), lambda i: (jnp.maximum(i * hb - 1, 0), 0)),
            pl.BlockSpec((CONV_HALO, D), lambda i: (jnp.minimum((i + 1) * hb, last), 0)),
            pl.BlockSpec((TS, D), lambda i: (i, 0)),
            pl.BlockSpec((None, 1, 6 * D), lambda i: (i // (GROUP_ROWS // TS), 0, 0)),
            pl.BlockSpec((D // 128, CONV_WIDTH, 128), lambda i: (0, 0, 0)),
            pl.BlockSpec((D // 128, 1, 128), lambda i: (0, 0, 0)),
            pl.BlockSpec((1, D), lambda i: (0, 0)),
            pl.BlockSpec((1, D), lambda i: (0, 0)),
            pl.BlockSpec((D, D), lambda i: (0, 0)),
            pl.BlockSpec((1, D), lambda i: (0, 0)),
        ],
        out_specs=pl.BlockSpec((TS, D), lambda i: (i, 0)),
        scratch_shapes=[pltpu.VMEM((D // 128, TS + 2 * CONV_HALO, 128), F32), pltpu.VMEM((D // 128, TS, 128), F32)],
        compiler_params=_cparams(("parallel",)),
        name="conv_dw_ln_pw2",
    )(u, u, u, x, mod, w_dw.reshape(CONV_WIDTH, D // 128, 128).transpose(1, 0, 2), b_dw.reshape(D // 128, 1, 128),
      row(ln_g), row(ln_b), w_pw2.astype(BF16), row(b_pw2))


def _pool_kernel(x_ref, xp_ref, xn_ref, mod_ref, g_ref, w_ref, sc_ref, o_ref, ext_scr):
    i = pl.program_id(0)
    has_prev, has_next = _seq_tile_flags(i)
    shift, scale = _mod_slice(mod_ref, 0), _mod_slice(mod_ref, 1)
    g = g_ref[...]
    x = x_ref[...]
    h = _norm_mod(x, g, shift, scale)
    ext_scr[0:POOL_HALO, :] = jnp.where(has_prev, _norm_mod(xp_ref[...], g, shift, scale), 0.0)
    ext_scr[POOL_HALO:POOL_HALO + TS, :] = h
    ext_scr[POOL_HALO + TS:, :] = jnp.where(has_next, _norm_mod(xn_ref[...], g, shift, scale), 0.0)

    seq_len = jnp.where(i >= PROMPT_ROWS // TS, T_SAMPLE, T_PROMPT)
    t = (i * TS) % seq_len + lax.broadcasted_iota(jnp.int32, (TS, GW), 0)
    outs = []
    for gi, w in enumerate(POOL_WINDOWS):
        lanes = slice(gi * GW, (gi + 1) * GW)
        tot = ext_scr[pl.ds(POOL_HALO - w // 2, TS), lanes]
        for j in range(1, w):
            tot = tot + ext_scr[pl.ds(POOL_HALO - w // 2 + j, TS), lanes]
        cnt = jnp.minimum(t + w // 2, seq_len) - jnp.maximum(t - w // 2, 0)
        p = tot / cnt.astype(F32) - h[:, lanes]
        outs.append(jnp.dot(p.astype(BF16), w_ref[gi], preferred_element_type=F32))
    y = jnp.concatenate(outs, axis=-1) * sc_ref[...]
    o_ref[...] = x + _mod_slice(mod_ref, 2) * y


def _pool_layer(x, mod, g, w_grp, scale):
    hb = TS // POOL_HALO
    last = N_ROWS // POOL_HALO - 1
    return pl.pallas_call(
        _pool_kernel,
        out_shape=jax.ShapeDtypeStruct((N_ROWS, D), F32),
        grid=(N_ROWS // TS,),
        in_specs=[
            pl.BlockSpec((TS, D), lambda i: (i, 0)),
            pl.BlockSpec((POOL_HALO, D), lambda i: (jnp.maximum(i * hb - 1, 0), 0)),
            pl.BlockSpec((POOL_HALO, D), lambda i: (jnp.minimum((i + 1) * hb, last), 0)),
            pl.BlockSpec((None, 1, 6 * D), lambda i: (i // (GROUP_ROWS // TS), 0, 0)),
            pl.BlockSpec((1, D), lambda i: (0, 0)),
            pl.BlockSpec((N_CGROUPS, GW, GW), lambda i: (0, 0, 0)),
            pl.BlockSpec((1, D), lambda i: (0, 0)),
        ],
        out_specs=pl.BlockSpec((TS, D), lambda i: (i, 0)),
        scratch_shapes=[pltpu.VMEM((TS + 2 * POOL_HALO, D), F32)],
        compiler_params=_cparams(("parallel",)),
        name="pool_mixer",
    )(x, x, x, mod, g, w_grp.astype(BF16), scale.reshape(1, D))


def _chan_dft_kernel(x_ref, mod_ref, g_ref, c_ref, s_ref, a_ref, b_ref):
    h = _norm_mod(x_ref[...], g_ref[...], _mod_slice(mod_ref, 0), _mod_slice(mod_ref, 1)).astype(BF16)
    c, s = c_ref[...], s_ref[...]
    for gi in range(N_CGROUPS):
        lanes = slice(gi * GW, (gi + 1) * GW)
        a_ref[:, lanes] = jnp.dot(h[:, lanes], c, preferred_element_type=F32).astype(a_ref.dtype)
        b_ref[:, lanes] = jnp.dot(h[:, lanes], s, preferred_element_type=F32).astype(b_ref.dtype)


def _dft_tables(n):
    idx = jnp.arange(n, dtype=jnp.int32)
    ang = (2.0 * np.pi / n) * ((idx[:, None] * idx[None, :]) % n).astype(F32)
    return jnp.cos(ang), jnp.sin(ang)


def _chan_dft(x, mod, g):
    c, s = _dft_tables(GW)
    return pl.pallas_call(
        _chan_dft_kernel,
        out_shape=[jax.ShapeDtypeStruct((N_ROWS, D), BF16)] * 2,
        grid=(N_ROWS // TM,),
        in_specs=[
            pl.BlockSpec((TM, D), lambda i: (i, 0)),
            pl.BlockSpec((None, 1, 6 * D), lambda i: (i // TILES_PER_GROUP, 0, 0)),
            pl.BlockSpec((1, D), lambda i: (0, 0)),
            pl.BlockSpec((GW, GW), lambda i: (0, 0)),
            pl.BlockSpec((GW, GW), lambda i: (0, 0)),
        ],
        out_specs=[pl.BlockSpec((TM, D), lambda i: (i, 0))] * 2,
        compiler_params=_cparams(("parallel",)),
        name="fnet_chan_dft",
    )(x, mod, g, c.astype(BF16), (-s).astype(BF16))


def _time_dft_kernel(c_ref, sn_ref, a_ref, b_ref, o_ref, acc_scr, *, n_k, scale):
    k = pl.program_id(2)

    @pl.when(k == 0)
    def _():
        acc_scr[...] = jnp.zeros_like(acc_scr)

    acc_scr[...] += (jnp.dot(c_ref[...], a_ref[...], preferred_element_type=F32)
                     + jnp.dot(sn_ref[...], b_ref[...], preferred_element_type=F32))

    @pl.when(k == n_k - 1)
    def _():
        o_ref[...] = (acc_scr[...] * scale).astype(o_ref.dtype)


def _time_dft(cos_t, sin_t, a, b, *, row0, n_seq, seq_len, name):
    tm = min(seq_len, 1024)
    tk = min(seq_len, 512)
    n_i, n_k = seq_len // tm, seq_len // tk
    kb0 = row0 // tk
    scale = 1.0 / math.sqrt(seq_len * GW)
    return pl.pallas_call(
        functools.partial(_time_dft_kernel, n_k=n_k, scale=scale),
        out_shape=jax.ShapeDtypeStruct((n_seq * seq_len, D), BF16),
        grid=(n_seq, n_i, n_k),
        in_specs=[
            pl.BlockSpec((tm, tk), lambda s, i, k: (i, k)),
            pl.BlockSpec((tm, tk), lambda s, i, k: (i, k)),
            pl.BlockSpec((tk, D), lambda s, i, k: (kb0 + s * n_k + k, 0)),
            pl.BlockSpec((tk, D), lambda s, i, k: (kb0 + s * n_k + k, 0)),
        ],
        out_specs=pl.BlockSpec((tm, D), lambda s, i, k: (s * n_i + i, 0)),
        scratch_shapes=[pltpu.VMEM((tm, D), F32)],
        compiler_params=_cparams(("parallel", "parallel", "arbitrary")),
        name=name,
    )(cos_t, sin_t, a, b)


def _cmul_const(xr, xi, wr, wi):
    def scaled(v, s):
        if s == 0.0:
            return None
        return v if s == 1.0 else (-v if s == -1.0 else v * s)

    def add(p, q):
        if p is None:
            return q
        return p if q is None else p + q

    return add(scaled(xr, wr), scaled(xi, -wi)), add(scaled(xi, wr), scaled(xr, wi))


def _fft_slabs(xr, xi):
    n = len(xr)
    if n == 1:
        return xr, xi
    er, ei = _fft_slabs(xr[0::2], xi[0::2])
    dr, di = _fft_slabs(xr[1::2], xi[1::2])
    out_r, out_i = [None] * n, [None] * n
    for k in range(n // 2):
        wr = float(round(math.cos(2.0 * math.pi * k / n), 15))
        wi = float(round(-math.sin(2.0 * math.pi * k / n), 15))
        tr, ti = _cmul_const(dr[k], di[k], wr, wi)
        out_r[k], out_i[k] = er[k] + tr, ei[k] + ti
        out_r[k + n // 2], out_i[k + n // 2] = er[k] - tr, ei[k] - ti
    return out_r, out_i


FFT_N1 = 8
FFT_N2 = T_SAMPLE // FFT_N1
FFT_LANES = 256
FFT_ROWS = 16


def _time_fft_kernel(a_ref, b_ref, twc_ref, tws_ref, m_ref, o_ref, z_scr, o_scr, *, scale):
    n_slabs = FFT_LANES // 128
    for lt in range(n_slabs):
        lanes = slice(lt * 128, (lt + 1) * 128)

        def body(j, carry, lanes=lanes):
            r0 = pl.multiple_of(j * FFT_ROWS, FFT_ROWS)
            xr = [a_ref[pl.ds(s * FFT_N2 + r0, FFT_ROWS), lanes].astype(F32) for s in range(FFT_N1)]
            xi = [b_ref[pl.ds(s * FFT_N2 + r0, FFT_ROWS), lanes].astype(F32) for s in range(FFT_N1)]
            yr, yi = _fft_slabs(xr, xi)
            for c in range(FFT_N1):
                if c == 0:
                    zr, zi = yr[c], yi[c]
                else:
                    tc = twc_ref[c, pl.ds(r0, FFT_ROWS), :]
                    ts = tws_ref[c, pl.ds(r0, FFT_ROWS), :]
                    zr = yr[c] * tc + yi[c] * ts
                    zi = yi[c] * tc - yr[c] * ts
                z_scr[c, pl.ds(r0, FFT_ROWS), lanes] = zr.astype(BF16)
                z_scr[c, pl.ds(FFT_N2 + r0, FFT_ROWS), lanes] = zi.astype(BF16)
            return carry

        lax.fori_loop(0, FFT_N2 // FFT_ROWS, body, 0)

    m = m_ref[...]
    for c in range(FFT_N1):
        r = jnp.dot(m, z_scr[c], preferred_element_type=F32) * scale
        for lt in range(n_slabs):
            o_scr[lt, pl.ds(c, FFT_N2, stride=FFT_N1), :] = r[:, lt * 128:(lt + 1) * 128]
    for lt in range(n_slabs):
        o_ref[:, lt * 128:(lt + 1) * 128] = o_scr[lt].astype(o_ref.dtype)


def _time_fft_sample(a, b):
    bidx = jnp.arange(FFT_N2, dtype=jnp.int32)
    cidx = jnp.arange(FFT_N1, dtype=jnp.int32)
    ang_tw = (2.0 * np.pi / T_SAMPLE) * (cidx[:, None] * bidx[None, :]).astype(F32)
    twc = jnp.broadcast_to(jnp.cos(ang_tw)[:, :, None], (FFT_N1, FFT_N2, 128))
    tws = jnp.broadcast_to(jnp.sin(ang_tw)[:, :, None], (FFT_N1, FFT_N2, 128))
    c2, s2 = _dft_tables(FFT_N2)
    m = jnp.concatenate([c2, s2], axis=1).astype(BF16)
    rb0 = PROMPT_ROWS // T_SAMPLE
    return pl.pallas_call(
        functools.partial(_time_fft_kernel, scale=1.0 / math.sqrt(T_SAMPLE * GW)),
        out_shape=jax.ShapeDtypeStruct((N_SAMPLE_SEQ * T_SAMPLE, D), BF16),
        grid=(N_SAMPLE_SEQ, D // FFT_LANES),
        in_specs=[
            pl.BlockSpec((T_SAMPLE, FFT_LANES), lambda s, l: (rb0 + s, l)),
            pl.BlockSpec((T_SAMPLE, FFT_LANES), lambda s, l: (rb0 + s, l)),
            pl.BlockSpec((FFT_N1, FFT_N2, 128), lambda s, l: (0, 0, 0)),
            pl.BlockSpec((FFT_N1, FFT_N2, 128), lambda s, l: (0, 0, 0)),
            pl.BlockSpec((FFT_N2, 2 * FFT_N2), lambda s, l: (0, 0)),
        ],
        out_specs=pl.BlockSpec((T_SAMPLE, FFT_LANES), lambda s, l: (s, l)),
        scratch_shapes=[pltpu.VMEM((FFT_N1, 2 * FFT_N2, FFT_LANES), BF16),
                        pltpu.VMEM((FFT_LANES // 128, T_SAMPLE, 128), F32)],
        compiler_params=_cparams(("parallel", "parallel")),
        name="fnet_time_fft_sample",
    )(a, b, twc, tws, m)


def _fourier_layer(x, mod, g, w, b):
    a, bn = _chan_dft(x, mod, g)
    cp, sp = _dft_tables(T_PROMPT)
    fp = _time_dft(cp.astype(BF16), sp.astype(BF16), a, bn,
                   row0=0, n_seq=N_PROMPT_SEQ, seq_len=T_PROMPT, name="fnet_time_dft_prompt")
    fs = _time_fft_sample(a, bn)
    return _out_proj(fp, fs, w.astype(BF16), b, x, mod, 2, "fnet_out_proj")


def kernel(x_prompt, x_sample, state_ret_fwd, state_ret_bwd, c, c_ctx, w_mod, b_mod, norm_mix_g, norm_mlp_g, mlp_w1, mlp_w2, ret_w_in, ret_gn_g, ret_w_out, ret_decay_fwd, ret_decay_bwd, conv_w_pw1, conv_b_pw1, conv_w_dw, conv_b_dw, conv_ln_g, conv_ln_b, conv_w_pw2, conv_b_pw2, pool_w, pool_scale, fnet_w, fnet_b, final_norm_g):
    depth = w_mod.shape[0]
    x = (x_prompt.reshape(PROMPT_ROWS, D), x_sample.reshape(N_SAMPLE_SEQ * T_SAMPLE, D))
    cond = jnp.concatenate([c_ctx[None, :], c, jnp.zeros((MOD_ROWS - N_GROUPS, D), F32)], axis=0)
    mod_all = _adaln_all(cond, w_mod, b_mod).reshape(depth, MOD_ROWS, 1, 6 * D)
    final_g = final_norm_g.reshape(1, D)
    new_f, new_b = [], []
    y_prompt = y_sample = None
    for i in range(depth):
        kind, j = i % 4, i // 4
        mod = mod_all[i]
        g_mix = norm_mix_g[i].reshape(1, D)
        if kind != 0 and isinstance(x, tuple):
            x = jnp.concatenate(x, axis=0)
        if kind == 0:
            xp, xs = x if isinstance(x, tuple) else (x[:PROMPT_ROWS], x[PROMPT_ROWS:])
            x, sf, sb = _retention_layer(xp, xs, mod, g_mix, ret_w_in[j], ret_gn_g[j], ret_w_out[j],
                                         ret_decay_fwd[j], ret_decay_bwd[j],
                                         (state_ret_fwd, j), (state_ret_bwd, j))
            new_f.append(sf)
            new_b.append(sb)
        elif kind == 1:
            x = _conv_layer(x, mod, g_mix, conv_w_pw1[j], conv_b_pw1[j], conv_w_dw[j], conv_b_dw[j],
                            conv_ln_g[j], conv_ln_b[j], conv_w_pw2[j], conv_b_pw2[j])
        elif kind == 2:
            x = _pool_layer(x, mod, g_mix, pool_w[j], pool_scale[j])
        else:
            x = _fourier_layer(x, mod, g_mix, fnet_w[j], fnet_b[j])
        g_mlp = norm_mlp_g[i].reshape(1, D)
        w1, w2 = mlp_w1[i].astype(BF16), mlp_w2[i].astype(BF16)
        if i == depth - 1:
            npt = PROMPT_ROWS // TM
            y_prompt = _mlp(x, mod, g_mlp, w1, w2, final_g, 0, npt, True, "mlp_final_prompt")
            y_sample = _mlp(x, mod, g_mlp, w1, w2, final_g, npt, N_ROWS // TM - npt, True, "mlp_final_sample")
        else:
            x = _mlp(x, mod, g_mlp, w1, w2, final_g, 0, N_ROWS // TM, False, "mlp")
    return (y_prompt.reshape(N_PROMPT_SEQ, T_PROMPT, D),
            y_sample.reshape(N_SAMPLE_SEQ, T_SAMPLE, D),
            jnp.stack(new_f, axis=1),
            jnp.stack(new_b, axis=1))
```

```python
import functools
import math

import numpy as np
import jax
import jax.numpy as jnp
from jax import lax
from jax.experimental import pallas as pl
from jax.experimental.pallas import tpu as pltpu

F32 = jnp.float32
BF16 = jnp.bfloat16

D = 1024
D_FF = 4 * D
N_PROMPT_SEQ = 16
T_PROMPT = 256
N_SAMPLE_SEQ = 4
T_SAMPLE = 4096
GROUP_ROWS = 4096
N_GROUPS = 1 + N_SAMPLE_SEQ
N_ROWS = N_GROUPS * GROUP_ROWS
PROMPT_ROWS = N_PROMPT_SEQ * T_PROMPT
MOD_ROWS = 16
GRID_W = 64
HEADS = 4
DK = 256
DV = 512
HK = HEADS * DK
HV = HEADS * DV
RET_CHUNK = 256
ROPE_BASE = 10000.0
CONV_WIDTH = 31
CONV_PAD = CONV_WIDTH // 2
CONV_HALO = 16
POOL_WINDOWS = (2, 4, 8, 16)
POOL_HALO = 8
POOL_PAD = 16
GW = 256
N_CGROUPS = D // GW
NORM_EPS = 1e-6
GN_EPS = 1e-5

TM = 1024
TILES_PER_GROUP = GROUP_ROWS // TM
TS = 256
VMEM_LIMIT = 56 * 1024 * 1024


def _cparams(sem):
    return pltpu.CompilerParams(dimension_semantics=sem, vmem_limit_bytes=VMEM_LIMIT)


def _norm_mod(x, g, shift, scale):
    ms = jnp.mean(x * x, axis=-1, keepdims=True)
    y = (x * lax.rsqrt(ms + NORM_EPS)) * g
    return y * (1.0 + scale) + shift


def _mod_slice(mod_ref, idx):
    return mod_ref[:, idx * D:(idx + 1) * D]


def _silu(x):
    return x * jax.nn.sigmoid(x)


def _mod_kernel(cond_ref, w_ref, b_ref, o_ref):
    a = _silu(cond_ref[...]).astype(BF16)
    o_ref[...] = jnp.dot(a, w_ref[...].astype(BF16), preferred_element_type=F32) + b_ref[...]


def _adaln_all(cond, w_mod, b_mod):
    depth = w_mod.shape[0]
    tn = 1536
    return pl.pallas_call(
        _mod_kernel,
        out_shape=jax.ShapeDtypeStruct((depth, MOD_ROWS, 6 * D), F32),
        grid=(depth, 6 * D // tn),
        in_specs=[
            pl.BlockSpec((MOD_ROWS, D), lambda l, j: (0, 0)),
            pl.BlockSpec((None, D, tn), lambda l, j: (l, 0, j)),
            pl.BlockSpec((None, 1, tn), lambda l, j: (l, 0, j)),
        ],
        out_specs=pl.BlockSpec((None, MOD_ROWS, tn), lambda l, j: (l, 0, j)),
        compiler_params=_cparams(("parallel", "parallel")),
        name="adaln_params",
    )(cond, w_mod, b_mod.reshape(depth, 1, 6 * D))


def _resident_spec(shape):
    return pl.BlockSpec(shape, lambda *_: (0,) * len(shape), pipeline_mode=pl.Buffered(1))


def _glu_kernel(x_ref, mod_ref, g_ref, w_ref, b_ref, o_ref):
    h = _norm_mod(x_ref[...], g_ref[...], _mod_slice(mod_ref, 0), _mod_slice(mod_ref, 1)).astype(BF16)
    a = jnp.dot(h, w_ref[:, :D], preferred_element_type=F32) + b_ref[:, :D]
    gt = jnp.dot(h, w_ref[:, D:], preferred_element_type=F32) + b_ref[:, D:]
    o_ref[...] = a * jax.nn.sigmoid(gt)


def _glu_proj(x, mod, g, w, b):
    return pl.pallas_call(
        _glu_kernel,
        out_shape=jax.ShapeDtypeStruct((N_ROWS, D), F32),
        grid=(N_ROWS // TM,),
        in_specs=[
            pl.BlockSpec((TM, D), lambda i: (i, 0)),
            pl.BlockSpec((None, 1, 6 * D), lambda i: (i // TILES_PER_GROUP, 0, 0)),
            pl.BlockSpec((1, D), lambda i: (0, 0)),
            _resident_spec((D, 2 * D)),
            pl.BlockSpec((1, 2 * D), lambda i: (0, 0)),
        ],
        out_specs=pl.BlockSpec((TM, D), lambda i: (i, 0)),
        compiler_params=_cparams(("parallel",)),
        name="conv_pw1_glu",
    )(x, mod, g, w, b.reshape(1, 2 * D))


def _out_kernel(yp_ref, ys_ref, w_ref, b_ref, xp_ref, *rest, gate_idx, n_prompt_tiles):
    xs_ref = rest[0] if len(rest) == 3 else xp_ref
    mod_ref, o_ref = rest[-2:]
    i = pl.program_id(0)
    w = w_ref[...]

    def finish(y_ref, x_ref):
        acc = jnp.dot(y_ref[...], w, preferred_element_type=F32) + b_ref[...]
        o_ref[...] = x_ref[...] + _mod_slice(mod_ref, gate_idx) * acc

    @pl.when(i < n_prompt_tiles)
    def _():
        finish(yp_ref, xp_ref)

    @pl.when(i >= n_prompt_tiles)
    def _():
        finish(ys_ref, xs_ref)


def _out_proj(y_prompt, y_sample, w, b, x, mod, gate_idx, name):
    k = w.shape[0]
    npt = PROMPT_ROWS // TM
    prompt_map = lambda i: (jnp.minimum(i, npt - 1), 0)
    sample_map = lambda i: (jnp.maximum(i - npt, 0), 0)
    if isinstance(x, tuple):
        x_args = list(x)
        x_specs = [pl.BlockSpec((TM, D), prompt_map), pl.BlockSpec((TM, D), sample_map)]
    else:
        x_args = [x]
        x_specs = [pl.BlockSpec((TM, D), lambda i: (i, 0))]
    return pl.pallas_call(
        functools.partial(_out_kernel, gate_idx=gate_idx, n_prompt_tiles=npt),
        out_shape=jax.ShapeDtypeStruct((N_ROWS, D), F32),
        grid=(N_ROWS // TM,),
        in_specs=[
            pl.BlockSpec((TM, k), prompt_map),
            pl.BlockSpec((TM, k), sample_map),
            pl.BlockSpec((k, D), lambda i: (0, 0)),
            pl.BlockSpec((1, D), lambda i: (0, 0)),
        ] + x_specs + [
            pl.BlockSpec((None, 1, 6 * D), lambda i: (i // TILES_PER_GROUP, 0, 0)),
        ],
        out_specs=pl.BlockSpec((TM, D), lambda i: (i, 0)),
        compiler_params=_cparams(("parallel",)),
        name=name,
    )(y_prompt, y_sample, w, b.reshape(1, D), *x_args, mod)


def _mlp_kernel(x_ref, mod_ref, g_ref, w1_ref, w2_ref, fg_ref, o_ref, *, tf, final_norm):
    x = x_ref[...]
    h = _norm_mod(x, g_ref[...], _mod_slice(mod_ref, 3), _mod_slice(mod_ref, 4)).astype(BF16)
    acc = None
    for f0 in range(0, D_FF, tf):
        hid = jnp.maximum(jnp.dot(h, w1_ref[:, f0:f0 + tf], preferred_element_type=F32), 0.0)
        part = jnp.dot((hid * hid).astype(BF16), w2_ref[f0:f0 + tf, :], preferred_element_type=F32)
        acc = part if acc is None else acc + part
    y = x + _mod_slice(mod_ref, 5) * acc
    if final_norm:
        ms = jnp.mean(y * y, axis=-1, keepdims=True)
        y = (y * lax.rsqrt(ms + NORM_EPS)) * fg_ref[...]
    o_ref[...] = y


def _mlp(x, mod, g, w1, w2, layer, final_g, row_tile0, n_tiles, final_norm, name):
    layer_spec = lambda r, c: pl.BlockSpec((None, r, c), lambda i: (layer, 0, 0), pipeline_mode=pl.Buffered(1))
    return pl.pallas_call(
        functools.partial(_mlp_kernel, tf=1024, final_norm=final_norm),
        out_shape=jax.ShapeDtypeStruct((n_tiles * TM, D), F32),
        grid=(n_tiles,),
        in_specs=[
            pl.BlockSpec((TM, D), lambda i: (i + row_tile0, 0)),
            pl.BlockSpec((None, 1, 6 * D), lambda i: ((i + row_tile0) // TILES_PER_GROUP, 0, 0)),
            pl.BlockSpec((1, D), lambda i: (0, 0)),
            layer_spec(D, D_FF),
            layer_spec(D_FF, D),
            pl.BlockSpec((1, D), lambda i: (0, 0)),
        ],
        out_specs=pl.BlockSpec((TM, D), lambda i: (i, 0)),
        compiler_params=_cparams(("parallel",)),
        name=name,
    )(x, mod, g, w1, w2, final_g)


def _scan_kernel(*refs, chunk, n_chunks, reverse, has_s0, has_zin, emit_state):
    refs = list(refs)
    p_ref, q_ref, k_ref, v_ref, g_ref, gn_ref = refs[:6]
    pos = 6
    s0_ref = zin_ref = sout_ref = None
    if has_s0:
        s0_ref = refs[pos]; pos += 1
    if has_zin:
        zin_ref = refs[pos]; pos += 1
    z_ref = refs[pos]; pos += 1
    if emit_state:
        sout_ref = refs[pos]; pos += 1
    s_scr, d_scr, xi_scr, zeta_scr = refs[pos:pos + 4]

    c = pl.program_id(1)
    log_g = [jnp.log1p(-jnp.exp2(-p_ref[h]))[:, :1] for h in range(HEADS)]

    @pl.when(c == 0)
    def _():
        ri = lax.broadcasted_iota(jnp.int32, (chunk, chunk), 0)
        ci = lax.broadcasted_iota(jnp.int32, (chunk, chunk), 1)
        rel = (ci - ri) if reverse else (ri - ci)
        relf = jnp.maximum(rel, 0).astype(F32)
        t = lax.broadcasted_iota(jnp.int32, (chunk, 128), 0)
        step = ((chunk - 1) - t if reverse else t).astype(F32)
        for h in range(HEADS):
            d_scr[h] = jnp.where(rel >= 0, jnp.exp(log_g[h] * relf), 0.0)
            xi_scr[h] = jnp.exp(log_g[h] * (step + 1.0))
            zeta_scr[h] = jnp.exp(log_g[h] * ((chunk - 1.0) - step))
        if has_s0:
            s_scr[...] = s0_ref[...]
        else:
            s_scr[...] = jnp.zeros_like(s_scr)

    for h in range(HEADS):
        kcols = slice(h * DK, (h + 1) * DK)
        vcols = slice(h * DV, (h + 1) * DV)
        q = q_ref[:, kcols]
        k = k_ref[:, kcols]
        v = v_ref[:, vcols]
        s = s_scr[h]
        scores = lax.dot_general(q, k, (((1,), (1,)), ((), ())), preferred_element_type=F32)
        inner = jnp.dot((scores * d_scr[h]).astype(BF16), v, preferred_element_type=F32)
        cross = jnp.dot(q, s.astype(BF16), preferred_element_type=F32)
        xi = jnp.concatenate([xi_scr[h]] * (DV // 128), axis=-1)
        o = inner + cross * xi
        zeta = jnp.concatenate([zeta_scr[h]] * (DK // 128), axis=-1)
        kz = (k.astype(F32) * zeta).astype(BF16)
        upd = lax.dot_general(kz, v, (((0,), (0,)), ((), ())), preferred_element_type=F32)
        s_scr[h] = s * jnp.exp(log_g[h] * float(chunk)) + upd

        mu = jnp.mean(o, axis=-1, keepdims=True)
        dlt = o - mu
        var = jnp.mean(dlt * dlt, axis=-1, keepdims=True)
        z = g_ref[:, vcols].astype(F32) * ((dlt * lax.rsqrt(var + GN_EPS)) * gn_ref[:, vcols])
        if has_zin:
            z = zin_ref[:, vcols] + z
        z_ref[:, vcols] = z.astype(z_ref.dtype)

    if emit_state:
        @pl.when(c == n_chunks - 1)
        def _():
            sout_ref[...] = s_scr[...]


def _scan(decay_p, proj, gn_g, s0, zin, *, row0, n_seq, seq_len, reverse, emit_state, out_dtype, name):
    chunk = RET_CHUNK
    nc = seq_len // chunk
    rb0 = row0 // chunk

    def rb(b, c):
        cc = (nc - 1 - c) if reverse else c
        return b * nc + cc

    in_specs = [
        pl.BlockSpec((HEADS, 1, 128), lambda b, c: (0, 0, 0)),
        pl.BlockSpec((chunk, HK), lambda b, c: (rb0 + rb(b, c), 0)),
        pl.BlockSpec((chunk, HK), lambda b, c: (rb0 + rb(b, c), 1)),
        pl.BlockSpec((chunk, HV), lambda b, c: (rb0 + rb(b, c), 1)),
        pl.BlockSpec((chunk, HV), lambda b, c: (rb0 + rb(b, c), 3 if reverse else 2)),
        pl.BlockSpec((1, HV), lambda b, c: (0, 0)),
    ]
    args = [decay_p, proj, proj, proj, proj, gn_g]
    if s0 is not None:
        s0_arr, s0_layer = s0
        in_specs.append(pl.BlockSpec((None, None, HEADS, DK, DV), lambda b, c: (b, s0_layer, 0, 0, 0)))
        args.append(s0_arr)
    if zin is not None:
        in_specs.append(pl.BlockSpec((chunk, HV), lambda b, c: (rb(b, c), 0)))
        args.append(zin)
    out_shape = [jax.ShapeDtypeStruct((n_seq * seq_len, HV), out_dtype)]
    out_specs = [pl.BlockSpec((chunk, HV), lambda b, c: (rb(b, c), 0))]
    if emit_state:
        out_shape.append(jax.ShapeDtypeStruct((n_seq, HEADS, DK, DV), F32))
        out_specs.append(pl.BlockSpec((None, HEADS, DK, DV), lambda b, c: (b, 0, 0, 0)))
    res = pl.pallas_call(
        functools.partial(_scan_kernel, chunk=chunk, n_chunks=nc, reverse=reverse,
                          has_s0=s0 is not None, has_zin=zin is not None, emit_state=emit_state),
        out_shape=out_shape,
        grid=(n_seq, nc),
        in_specs=in_specs,
        out_specs=out_specs,
        scratch_shapes=[pltpu.VMEM((HEADS, DK, DV), F32), pltpu.VMEM((HEADS, chunk, chunk), F32),
                        pltpu.VMEM((HEADS, chunk, 128), F32), pltpu.VMEM((HEADS, chunk, 128), F32)],
        compiler_params=_cparams(("parallel", "arbitrary")),
        name=name,
    )(*args)
    return res if emit_state else (res[0], None)


def _rope_tables():
    n = DK // 4
    inv = ROPE_BASE ** (-jnp.arange(n, dtype=F32) / n)
    t = jnp.arange(T_SAMPLE, dtype=jnp.int32)
    row = (t // GRID_W).astype(F32)
    col = (t % GRID_W).astype(F32)
    ang = jnp.concatenate([row[:, None] * inv[None, :], col[:, None] * inv[None, :]], axis=-1)
    cos = jnp.concatenate([jnp.ones((PROMPT_ROWS, DK // 2), F32)] + [jnp.cos(ang)] * N_SAMPLE_SEQ, axis=0)
    sin = jnp.concatenate([jnp.zeros((PROMPT_ROWS, DK // 2), F32)] + [jnp.sin(ang)] * N_SAMPLE_SEQ, axis=0)
    return cos, sin


def _ret_proj_kernel(xp_ref, xs_ref, mod_ref, g_ref, w_ref, cos_ref, sin_ref, o_ref, *, n_prompt_tiles):
    x = jnp.where(pl.program_id(0) < n_prompt_tiles, xp_ref[...], xs_ref[...])
    h = _norm_mod(x, g_ref[...], _mod_slice(mod_ref, 0), _mod_slice(mod_ref, 1)).astype(BF16)
    half = DK // 2
    cos, sin = cos_ref[...], sin_ref[...]
    for j in range(w_ref.shape[1] // HK):
        acc = jnp.dot(h, w_ref[:, j * HK:(j + 1) * HK], preferred_element_type=F32)
        if j < 2:
            kscale = DK ** -0.5 if j == 1 else 1.0
            for hd in range(HEADS):
                c0 = j * HK + hd * DK
                x1 = acc[:, hd * DK:hd * DK + half]
                x2 = acc[:, hd * DK + half:(hd + 1) * DK]
                o_ref[:, c0:c0 + half] = ((x1 * cos - x2 * sin) * kscale).astype(o_ref.dtype)
                o_ref[:, c0 + half:c0 + DK] = ((x2 * cos + x1 * sin) * kscale).astype(o_ref.dtype)
        else:
            if j * HK >= 2 * HK + HV:
                acc = _silu(acc)
            o_ref[:, j * HK:(j + 1) * HK] = acc.astype(o_ref.dtype)


def _ret_proj(xp, xs, mod, g, w_in, cos, sin):
    tm = TM // 2
    npt = PROMPT_ROWS // tm
    half = DK // 2
    n_cols = w_in.shape[1]
    return pl.pallas_call(
        functools.partial(_ret_proj_kernel, n_prompt_tiles=npt),
        out_shape=jax.ShapeDtypeStruct((N_ROWS, n_cols), BF16),
        grid=(N_ROWS // tm,),
        in_specs=[
            pl.BlockSpec((tm, D), lambda i: (jnp.minimum(i, npt - 1), 0)),
            pl.BlockSpec((tm, D), lambda i: (jnp.maximum(i - npt, 0), 0)),
            pl.BlockSpec((None, 1, 6 * D), lambda i: (i // (GROUP_ROWS // tm), 0, 0)),
            pl.BlockSpec((1, D), lambda i: (0, 0)),
            _resident_spec((D, n_cols)),
            pl.BlockSpec((tm, half), lambda i: (i, 0)),
            pl.BlockSpec((tm, half), lambda i: (i, 0)),
        ],
        out_specs=pl.BlockSpec((tm, n_cols), lambda i: (i, 0)),
        compiler_params=_cparams(("parallel",)),
        name="ret_in_proj",
    )(xp, xs, mod, g, w_in, cos, sin)


def _retention_layer(xp, xs, mod, g, w_in, gn_g, w_out, decay_fwd, decay_bwd, s0_fwd, s0_bwd):
    cos, sin = _rope_tables()
    proj = _ret_proj(xp, xs, mod, g, w_in.astype(BF16), cos, sin)
    gn = gn_g.reshape(1, HV)
    pf = jnp.broadcast_to(decay_fwd.astype(F32)[:, None, None], (HEADS, 1, 128))
    pb = jnp.broadcast_to(decay_bwd.astype(F32)[:, None, None], (HEADS, 1, 128))
    prompt = dict(row0=0, n_seq=N_PROMPT_SEQ, seq_len=T_PROMPT, emit_state=True)
    sample = dict(row0=PROMPT_ROWS, n_seq=N_SAMPLE_SEQ, seq_len=T_SAMPLE, emit_state=False)
    zp, sf = _scan(pf, proj, gn, None, None, reverse=False, out_dtype=F32, name="ret_scan_prompt_fwd", **prompt)
    yp, sb = _scan(pb, proj, gn, None, zp, reverse=True, out_dtype=BF16, name="ret_scan_prompt_bwd", **prompt)
    zs, _ = _scan(pf, proj, gn, s0_fwd, None, reverse=False, out_dtype=F32, name="ret_scan_sample_fwd", **sample)
    ys, _ = _scan(pb, proj, gn, s0_bwd, zs, reverse=True, out_dtype=BF16, name="ret_scan_sample_bwd", **sample)
    x = _out_proj(yp, ys, w_out.astype(BF16), jnp.zeros((D,), F32), (xp, xs), mod, 2, "ret_out_proj")
    return x, sf, sb


def _seq_tile_flags(i):
    n_prompt_tiles = PROMPT_ROWS // TS
    tiles_per_seq = T_SAMPLE // TS
    is_sample = i >= n_prompt_tiles
    tin = (i - n_prompt_tiles) % tiles_per_seq
    if T_PROMPT != TS:
        raise NotImplementedError("prompt sequences must be exactly one row tile")
    return is_sample & (tin > 0), is_sample & (tin < tiles_per_seq - 1)


def _conv_kernel(u_ref, up_ref, un_ref, x_ref, mod_ref, wdw_ref, bdw_ref, lng_ref, lnb_ref, w2_ref, b2_ref,
                 o_ref, ext_scr, cv_scr):
    has_prev, has_next = _seq_tile_flags(pl.program_id(0))
    n_slabs = D // 128
    for lt in range(n_slabs):
        lanes = slice(lt * 128, (lt + 1) * 128)
        ext_scr[lt, 0:CONV_HALO, :] = jnp.where(has_prev, up_ref[:, lanes], 0.0)
        ext_scr[lt, CONV_HALO:CONV_HALO + TS, :] = u_ref[:, lanes]
        ext_scr[lt, CONV_HALO + TS:, :] = jnp.where(has_next, un_ref[:, lanes], 0.0)

    rows = 32
    shift = CONV_HALO - CONV_PAD

    def body(lt, carry):
        for r0 in range(0, TS, rows):
            acc = jnp.broadcast_to(bdw_ref[lt], (rows, 128))
            for kk in range(CONV_WIDTH):
                acc = acc + wdw_ref[lt, kk:kk + 1, :] * ext_scr[lt, r0 + kk + shift:r0 + kk + shift + rows, :]
            cv_scr[lt, r0:r0 + rows, :] = acc
        return carry

    lax.fori_loop(0, n_slabs, body, 0)

    cv = jnp.concatenate([cv_scr[lt] for lt in range(n_slabs)], axis=-1)
    mu = jnp.mean(cv, axis=-1, keepdims=True)
    dlt = cv - mu
    var = jnp.mean(dlt * dlt, axis=-1, keepdims=True)
    ln = (dlt * lax.rsqrt(var + GN_EPS)) * lng_ref[...] + lnb_ref[...]
    act = _silu(ln).astype(BF16)
    y = jnp.dot(act, w2_ref[...], preferred_element_type=F32) + b2_ref[...]
    o_ref[...] = x_ref[...] + _mod_slice(mod_ref, 2) * y


def _conv_layer(x, mod, g, w_pw1, b_pw1, w_dw, b_dw, ln_g, ln_b, w_pw2, b_pw2):
    u = _glu_proj(x, mod, g, w_pw1.astype(BF16), b_pw1)
    hb = TS // CONV_HALO
    last = N_ROWS // CONV_HALO - 1
    row = lambda a: a.reshape(1, D)
    return pl.pallas_call(
        _conv_kernel,
        out_shape=jax.ShapeDtypeStruct((N_ROWS, D), F32),
        grid=(N_ROWS // TS,),
        in_specs=[
            pl.BlockSpec((TS, D), lambda i: (i, 0)),
            pl.BlockSpec((CONV_HALO, D), lambda i: (jnp.maximum(i * hb - 1, 0), 0)),
            pl.BlockSpec((CONV_HALO, D), lambda i: (jnp.minimum((i + 1) * hb, last), 0)),
            pl.BlockSpec((TS, D), lambda i: (i, 0)),
            pl.BlockSpec((None, 1, 6 * D), lambda i: (i // (GROUP_ROWS // TS), 0, 0)),
            pl.BlockSpec((D // 128, CONV_WIDTH, 128), lambda i: (0, 0, 0)),
            pl.BlockSpec((D // 128, 1, 128), lambda i: (0, 0, 0)),
            pl.BlockSpec((1, D), lambda i: (0, 0)),
            pl.BlockSpec((1, D), lambda i: (0, 0)),
            pl.BlockSpec((D, D), lambda i: (0, 0)),
            pl.BlockSpec((1, D), lambda i: (0, 0)),
        ],
        out_specs=pl.BlockSpec((TS, D), lambda i: (i, 0)),
        scratch_shapes=[pltpu.VMEM((D // 128, TS + 2 * CONV_HALO, 128), F32), pltpu.VMEM((D // 128, TS, 128), F32)],
        compiler_params=_cparams(("parallel",)),
        name="conv_dw_ln_pw2",
    )(u, u, u, x, mod, w_dw.reshape(CONV_WIDTH, D // 128, 128).transpose(1, 0, 2), b_dw.reshape(D // 128, 1, 128),
      row(ln_g), row(ln_b), w_pw2.astype(BF16), row(b_pw2))


def _window_sum(e, w):
    q, span, n = e, 1, e.shape[0]
    while 2 * span < w:
        n -= 8
        q = q[0:n] + q[span:span + n]
        span *= 2
    start = POOL_HALO - w // 2
    return q[start:start + TS] + q[start + span:start + span + TS]


def _pool_kernel(x_ref, xp_ref, xn_ref, mod_ref, g_ref, w_ref, sc_ref, o_ref, ext_scr):
    i = pl.program_id(0)
    has_prev, has_next = _seq_tile_flags(i)
    shift, scale = _mod_slice(mod_ref, 0), _mod_slice(mod_ref, 1)
    g = g_ref[...]
    x = x_ref[...]
    h = _norm_mod(x, g, shift, scale)
    ext_scr[0:POOL_HALO, :] = jnp.where(has_prev, _norm_mod(xp_ref[...], g, shift, scale), 0.0)
    ext_scr[POOL_HALO:POOL_HALO + TS, :] = h
    ext_scr[POOL_HALO + TS:2 * POOL_HALO + TS, :] = jnp.where(has_next, _norm_mod(xn_ref[...], g, shift, scale), 0.0)
    ext_scr[2 * POOL_HALO + TS:, :] = jnp.zeros((POOL_PAD, D), F32)

    seq_len = jnp.where(i >= PROMPT_ROWS // TS, T_SAMPLE, T_PROMPT)
    t = (i * TS) % seq_len + lax.broadcasted_iota(jnp.int32, (TS, GW), 0)
    outs = []
    for gi, w in enumerate(POOL_WINDOWS):
        lanes = slice(gi * GW, (gi + 1) * GW)
        tot = _window_sum(ext_scr[:, lanes], w)
        cnt = jnp.minimum(t + w // 2, seq_len) - jnp.maximum(t - w // 2, 0)
        p = tot / cnt.astype(F32) - h[:, lanes]
        outs.append(jnp.dot(p.astype(BF16), w_ref[gi], preferred_element_type=F32))
    y = jnp.concatenate(outs, axis=-1) * sc_ref[...]
    o_ref[...] = x + _mod_slice(mod_ref, 2) * y


def _pool_layer(x, mod, g, w_grp, scale):
    hb = TS // POOL_HALO
    last = N_ROWS // POOL_HALO - 1
    return pl.pallas_call(
        _pool_kernel,
        out_shape=jax.ShapeDtypeStruct((N_ROWS, D), F32),
        grid=(N_ROWS // TS,),
        in_specs=[
            pl.BlockSpec((TS, D), lambda i: (i, 0)),
            pl.BlockSpec((POOL_HALO, D), lambda i: (jnp.maximum(i * hb - 1, 0), 0)),
            pl.BlockSpec((POOL_HALO, D), lambda i: (jnp.minimum((i + 1) * hb, last), 0)),
            pl.BlockSpec((None, 1, 6 * D), lambda i: (i // (GROUP_ROWS // TS), 0, 0)),
            pl.BlockSpec((1, D), lambda i: (0, 0)),
            pl.BlockSpec((N_CGROUPS, GW, GW), lambda i: (0, 0, 0)),
            pl.BlockSpec((1, D), lambda i: (0, 0)),
        ],
        out_specs=pl.BlockSpec((TS, D), lambda i: (i, 0)),
        scratch_shapes=[pltpu.VMEM((TS + 2 * POOL_HALO + POOL_PAD, D), F32)],
        compiler_params=_cparams(("parallel",)),
        name="pool_mixer",
    )(x, x, x, mod, g, w_grp.astype(BF16), scale.reshape(1, D))


def _chan_dft_kernel(x_ref, mod_ref, g_ref, c_ref, s_ref, a_ref, b_ref):
    h = _norm_mod(x_ref[...], g_ref[...], _mod_slice(mod_ref, 0), _mod_slice(mod_ref, 1)).astype(BF16)
    c, s = c_ref[...], s_ref[...]
    for gi in range(N_CGROUPS):
        lanes = slice(gi * GW, (gi + 1) * GW)
        a_ref[:, lanes] = jnp.dot(h[:, lanes], c, preferred_element_type=F32).astype(a_ref.dtype)
        b_ref[:, lanes] = jnp.dot(h[:, lanes], s, preferred_element_type=F32).astype(b_ref.dtype)


def _dft_tables(n):
    idx = jnp.arange(n, dtype=jnp.int32)
    ang = (2.0 * np.pi / n) * ((idx[:, None] * idx[None, :]) % n).astype(F32)
    return jnp.cos(ang), jnp.sin(ang)


def _chan_dft(x, mod, g):
    c, s = _dft_tables(GW)
    return pl.pallas_call(
        _chan_dft_kernel,
        out_shape=[jax.ShapeDtypeStruct((N_ROWS, D), BF16)] * 2,
        grid=(N_ROWS // TM,),
        in_specs=[
            pl.BlockSpec((TM, D), lambda i: (i, 0)),
            pl.BlockSpec((None, 1, 6 * D), lambda i: (i // TILES_PER_GROUP, 0, 0)),
            pl.BlockSpec((1, D), lambda i: (0, 0)),
            pl.BlockSpec((GW, GW), lambda i: (0, 0)),
            pl.BlockSpec((GW, GW), lambda i: (0, 0)),
        ],
        out_specs=[pl.BlockSpec((TM, D), lambda i: (i, 0))] * 2,
        compiler_params=_cparams(("parallel",)),
        name="fnet_chan_dft",
    )(x, mod, g, c.astype(BF16), (-s).astype(BF16))


def _time_dft_kernel(c_ref, sn_ref, a_ref, b_ref, o_ref, acc_scr, *, n_k, scale):
    k = pl.program_id(2)

    @pl.when(k == 0)
    def _():
        acc_scr[...] = jnp.zeros_like(acc_scr)

    acc_scr[...] += (jnp.dot(c_ref[...], a_ref[...], preferred_element_type=F32)
                     + jnp.dot(sn_ref[...], b_ref[...], preferred_element_type=F32))

    @pl.when(k == n_k - 1)
    def _():
        o_ref[...] = (acc_scr[...] * scale).astype(o_ref.dtype)


def _time_dft(cos_t, sin_t, a, b, *, row0, n_seq, seq_len, name):
    tm = min(seq_len, 1024)
    tk = min(seq_len, 512)
    n_i, n_k = seq_len // tm, seq_len // tk
    kb0 = row0 // tk
    scale = 1.0 / math.sqrt(seq_len * GW)
    return pl.pallas_call(
        functools.partial(_time_dft_kernel, n_k=n_k, scale=scale),
        out_shape=jax.ShapeDtypeStruct((n_seq * seq_len, D), BF16),
        grid=(n_seq, n_i, n_k),
        in_specs=[
            pl.BlockSpec((tm, tk), lambda s, i, k: (i, k)),
            pl.BlockSpec((tm, tk), lambda s, i, k: (i, k)),
            pl.BlockSpec((tk, D), lambda s, i, k: (kb0 + s * n_k + k, 0)),
            pl.BlockSpec((tk, D), lambda s, i, k: (kb0 + s * n_k + k, 0)),
        ],
        out_specs=pl.BlockSpec((tm, D), lambda s, i, k: (s * n_i + i, 0)),
        scratch_shapes=[pltpu.VMEM((tm, D), F32)],
        compiler_params=_cparams(("parallel", "parallel", "arbitrary")),
        name=name,
    )(cos_t, sin_t, a, b)


def _cmul_const(xr, xi, wr, wi):
    def scaled(v, s):
        if s == 0.0:
            return None
        return v if s == 1.0 else (-v if s == -1.0 else v * s)

    def add(p, q):
        if p is None:
            return q
        return p if q is None else p + q

    return add(scaled(xr, wr), scaled(xi, -wi)), add(scaled(xi, wr), scaled(xr, wi))


def _fft_slabs(xr, xi):
    n = len(xr)
    if n == 1:
        return xr, xi
    er, ei = _fft_slabs(xr[0::2], xi[0::2])
    dr, di = _fft_slabs(xr[1::2], xi[1::2])
    out_r, out_i = [None] * n, [None] * n
    for k in range(n // 2):
        wr = float(round(math.cos(2.0 * math.pi * k / n), 15))
        wi = float(round(-math.sin(2.0 * math.pi * k / n), 15))
        tr, ti = _cmul_const(dr[k], di[k], wr, wi)
        out_r[k], out_i[k] = er[k] + tr, ei[k] + ti
        out_r[k + n // 2], out_i[k + n // 2] = er[k] - tr, ei[k] - ti
    return out_r, out_i


FFT_N1 = 8
FFT_N2 = T_SAMPLE // FFT_N1
FFT_LANES = 256
FFT_ROWS = 16


def _time_fft_kernel(a_ref, b_ref, twc_ref, tws_ref, m_ref, o_ref, z_scr, o_scr, *, scale):
    n_slabs = FFT_LANES // 128
    for lt in range(n_slabs):
        lanes = slice(lt * 128, (lt + 1) * 128)

        def body(j, carry, lanes=lanes):
            r0 = pl.multiple_of(j * FFT_ROWS, FFT_ROWS)
            xr = [a_ref[pl.ds(s * FFT_N2 + r0, FFT_ROWS), lanes].astype(F32) for s in range(FFT_N1)]
            xi = [b_ref[pl.ds(s * FFT_N2 + r0, FFT_ROWS), lanes].astype(F32) for s in range(FFT_N1)]
            yr, yi = _fft_slabs(xr, xi)
            for c in range(FFT_N1):
                if c == 0:
                    zr, zi = yr[c], yi[c]
                else:
                    tc = twc_ref[c, pl.ds(r0, FFT_ROWS), :]
                    ts = tws_ref[c, pl.ds(r0, FFT_ROWS), :]
                    zr = yr[c] * tc + yi[c] * ts
                    zi = yi[c] * tc - yr[c] * ts
                z_scr[c, pl.ds(r0, FFT_ROWS), lanes] = zr.astype(BF16)
                z_scr[c, pl.ds(FFT_N2 + r0, FFT_ROWS), lanes] = zi.astype(BF16)
            return carry

        lax.fori_loop(0, FFT_N2 // FFT_ROWS, body, 0)

    m = m_ref[...]
    for c in range(FFT_N1):
        r = jnp.dot(m, z_scr[c], preferred_element_type=F32) * scale
        for lt in range(n_slabs):
            o_scr[lt, pl.ds(c, FFT_N2, stride=FFT_N1), :] = r[:, lt * 128:(lt + 1) * 128]
    for lt in range(n_slabs):
        o_ref[:, lt * 128:(lt + 1) * 128] = o_scr[lt].astype(o_ref.dtype)


def _time_fft_sample(a, b):
    bidx = jnp.arange(FFT_N2, dtype=jnp.int32)
    cidx = jnp.arange(FFT_N1, dtype=jnp.int32)
    ang_tw = (2.0 * np.pi / T_SAMPLE) * (cidx[:, None] * bidx[None, :]).astype(F32)
    twc = jnp.broadcast_to(jnp.cos(ang_tw)[:, :, None], (FFT_N1, FFT_N2, 128))
    tws = jnp.broadcast_to(jnp.sin(ang_tw)[:, :, None], (FFT_N1, FFT_N2, 128))
    c2, s2 = _dft_tables(FFT_N2)
    m = jnp.concatenate([c2, s2], axis=1).astype(BF16)
    rb0 = PROMPT_ROWS // T_SAMPLE
    return pl.pallas_call(
        functools.partial(_time_fft_kernel, scale=1.0 / math.sqrt(T_SAMPLE * GW)),
        out_shape=jax.ShapeDtypeStruct((N_SAMPLE_SEQ * T_SAMPLE, D), BF16),
        grid=(N_SAMPLE_SEQ, D // FFT_LANES),
        in_specs=[
            pl.BlockSpec((T_SAMPLE, FFT_LANES), lambda s, l: (rb0 + s, l)),
            pl.BlockSpec((T_SAMPLE, FFT_LANES), lambda s, l: (rb0 + s, l)),
            pl.BlockSpec((FFT_N1, FFT_N2, 128), lambda s, l: (0, 0, 0)),
            pl.BlockSpec((FFT_N1, FFT_N2, 128), lambda s, l: (0, 0, 0)),
            pl.BlockSpec((FFT_N2, 2 * FFT_N2), lambda s, l: (0, 0)),
        ],
        out_specs=pl.BlockSpec((T_SAMPLE, FFT_LANES), lambda s, l: (s, l)),
        scratch_shapes=[pltpu.VMEM((FFT_N1, 2 * FFT_N2, FFT_LANES), BF16),
                        pltpu.VMEM((FFT_LANES // 128, T_SAMPLE, 128), F32)],
        compiler_params=_cparams(("parallel", "parallel")),
        name="fnet_time_fft_sample",
    )(a, b, twc, tws, m)


def _fourier_layer(x, mod, g, w, b):
    a, bn = _chan_dft(x, mod, g)
    cp, sp = _dft_tables(T_PROMPT)
    fp = _time_dft(cp.astype(BF16), sp.astype(BF16), a, bn,
                   row0=0, n_seq=N_PROMPT_SEQ, seq_len=T_PROMPT, name="fnet_time_dft_prompt")
    fs = _time_fft_sample(a, bn)
    return _out_proj(fp, fs, w.astype(BF16), b, x, mod, 2, "fnet_out_proj")


def kernel(x_prompt, x_sample, state_ret_fwd, state_ret_bwd, c, c_ctx, w_mod, b_mod, norm_mix_g, norm_mlp_g, mlp_w1, mlp_w2, ret_w_in, ret_gn_g, ret_w_out, ret_decay_fwd, ret_decay_bwd, conv_w_pw1, conv_b_pw1, conv_w_dw, conv_b_dw, conv_ln_g, conv_ln_b, conv_w_pw2, conv_b_pw2, pool_w, pool_scale, fnet_w, fnet_b, final_norm_g):
    depth = w_mod.shape[0]
    x = (x_prompt.reshape(PROMPT_ROWS, D), x_sample.reshape(N_SAMPLE_SEQ * T_SAMPLE, D))
    cond = jnp.concatenate([c_ctx[None, :], c, jnp.zeros((MOD_ROWS - N_GROUPS, D), F32)], axis=0)
    mod_all = _adaln_all(cond, w_mod, b_mod).reshape(depth, MOD_ROWS, 1, 6 * D)
    final_g = final_norm_g.reshape(1, D)
    w1_all, w2_all = mlp_w1.astype(BF16), mlp_w2.astype(BF16)
    new_f, new_b = [], []
    y_prompt = y_sample = None
    for i in range(depth):
        kind, j = i % 4, i // 4
        mod = mod_all[i]
        g_mix = norm_mix_g[i].reshape(1, D)
        if kind != 0 and isinstance(x, tuple):
            x = jnp.concatenate(x, axis=0)
        if kind == 0:
            xp, xs = x if isinstance(x, tuple) else (x[:PROMPT_ROWS], x[PROMPT_ROWS:])
            x, sf, sb = _retention_layer(xp, xs, mod, g_mix, ret_w_in[j], ret_gn_g[j], ret_w_out[j],
                                         ret_decay_fwd[j], ret_decay_bwd[j],
                                         (state_ret_fwd, j), (state_ret_bwd, j))
            new_f.append(sf)
            new_b.append(sb)
        elif kind == 1:
            x = _conv_layer(x, mod, g_mix, conv_w_pw1[j], conv_b_pw1[j], conv_w_dw[j], conv_b_dw[j],
                            conv_ln_g[j], conv_ln_b[j], conv_w_pw2[j], conv_b_pw2[j])
        elif kind == 2:
            x = _pool_layer(x, mod, g_mix, pool_w[j], pool_scale[j])
        else:
            x = _fourier_layer(x, mod, g_mix, fnet_w[j], fnet_b[j])
        g_mlp = norm_mlp_g[i].reshape(1, D)
        if i == depth - 1:
            npt = PROMPT_ROWS // TM
            y_prompt = _mlp(x, mod, g_mlp, w1_all, w2_all, i, final_g, 0, npt, True, "mlp_final_prompt")
            y_sample = _mlp(x, mod, g_mlp, w1_all, w2_all, i, final_g, npt, N_ROWS // TM - npt, True,
                            "mlp_final_sample")
        else:
            x = _mlp(x, mod, g_mlp, w1_all, w2_all, i, final_g, 0, N_ROWS // TM, False, "mlp")
    return (y_prompt.reshape(N_PROMPT_SEQ, T_PROMPT, D),
            y_sample.reshape(N_SAMPLE_SEQ, T_SAMPLE, D),
            jnp.stack(new_f, axis=1),
            jnp.stack(new_b, axis=1))
```

```python
import functools
import math

import numpy as np
import jax
import jax.numpy as jnp
from jax import lax
from jax.experimental import pallas as pl
from jax.experimental.pallas import tpu as pltpu

F32 = jnp.float32
BF16 = jnp.bfloat16

D = 1024
D_FF = 4 * D
N_PROMPT_SEQ = 16
T_PROMPT = 256
N_SAMPLE_SEQ = 4
T_SAMPLE = 4096
GROUP_ROWS = 4096
N_GROUPS = 1 + N_SAMPLE_SEQ
N_ROWS = N_GROUPS * GROUP_ROWS
PROMPT_ROWS = N_PROMPT_SEQ * T_PROMPT
MOD_ROWS = 16
GRID_W = 64
HEADS = 4
DK = 256
DV = 512
HK = HEADS * DK
HV = HEADS * DV
RET_CHUNK = 256
ROPE_BASE = 10000.0
CONV_WIDTH = 31
CONV_PAD = CONV_WIDTH // 2
CONV_HALO = 16
POOL_WINDOWS = (2, 4, 8, 16)
POOL_HALO = 8
POOL_PAD = 16
GW = 256
N_CGROUPS = D // GW
NORM_EPS = 1e-6
GN_EPS = 1e-5

TM = 1024
TILES_PER_GROUP = GROUP_ROWS // TM
TS = 256
VMEM_LIMIT = 56 * 1024 * 1024


def _cparams(sem):
    return pltpu.CompilerParams(dimension_semantics=sem, vmem_limit_bytes=VMEM_LIMIT)


def _norm_mod(x, g, shift, scale):
    ms = jnp.mean(x * x, axis=-1, keepdims=True)
    y = (x * lax.rsqrt(ms + NORM_EPS)) * g
    return y * (1.0 + scale) + shift


def _mod_slice(mod_ref, idx):
    return mod_ref[:, idx * D:(idx + 1) * D]


def _silu(x):
    return x * jax.nn.sigmoid(x)


def _mod_kernel(cond_ref, w_ref, b_ref, o_ref):
    a = _silu(cond_ref[...]).astype(BF16)
    o_ref[...] = jnp.dot(a, w_ref[...].astype(BF16), preferred_element_type=F32) + b_ref[...]


def _adaln_all(cond, w_mod, b_mod):
    depth = w_mod.shape[0]
    tn = 1536
    return pl.pallas_call(
        _mod_kernel,
        out_shape=jax.ShapeDtypeStruct((depth, MOD_ROWS, 6 * D), F32),
        grid=(depth, 6 * D // tn),
        in_specs=[
            pl.BlockSpec((MOD_ROWS, D), lambda l, j: (0, 0)),
            pl.BlockSpec((None, D, tn), lambda l, j: (l, 0, j)),
            pl.BlockSpec((None, 1, tn), lambda l, j: (l, 0, j)),
        ],
        out_specs=pl.BlockSpec((None, MOD_ROWS, tn), lambda l, j: (l, 0, j)),
        compiler_params=_cparams(("parallel", "parallel")),
        name="adaln_params",
    )(cond, w_mod, b_mod.reshape(depth, 1, 6 * D))


def _resident_spec(shape):
    return pl.BlockSpec(shape, lambda *_: (0,) * len(shape), pipeline_mode=pl.Buffered(1))


def _glu_kernel(x_ref, mod_ref, g_ref, w_ref, b_ref, o_ref):
    h = _norm_mod(x_ref[...], g_ref[...], _mod_slice(mod_ref, 0), _mod_slice(mod_ref, 1)).astype(BF16)
    a = jnp.dot(h, w_ref[:, :D], preferred_element_type=F32) + b_ref[:, :D]
    gt = jnp.dot(h, w_ref[:, D:], preferred_element_type=F32) + b_ref[:, D:]
    o_ref[...] = a * jax.nn.sigmoid(gt)


def _glu_proj(x, mod, g, w, b):
    return pl.pallas_call(
        _glu_kernel,
        out_shape=jax.ShapeDtypeStruct((N_ROWS, D), F32),
        grid=(N_ROWS // TM,),
        in_specs=[
            pl.BlockSpec((TM, D), lambda i: (i, 0)),
            pl.BlockSpec((None, 1, 6 * D), lambda i: (i // TILES_PER_GROUP, 0, 0)),
            pl.BlockSpec((1, D), lambda i: (0, 0)),
            _resident_spec((D, 2 * D)),
            pl.BlockSpec((1, 2 * D), lambda i: (0, 0)),
        ],
        out_specs=pl.BlockSpec((TM, D), lambda i: (i, 0)),
        compiler_params=_cparams(("parallel",)),
        name="conv_pw1_glu",
    )(x, mod, g, w, b.reshape(1, 2 * D))


def _out_kernel(yp_ref, ys_ref, w_ref, b_ref, xp_ref, *rest, gate_idx, n_prompt_tiles):
    xs_ref = rest[0] if len(rest) == 3 else xp_ref
    mod_ref, o_ref = rest[-2:]
    i = pl.program_id(0)
    w = w_ref[...]

    def finish(y_ref, x_ref):
        acc = jnp.dot(y_ref[...], w, preferred_element_type=F32) + b_ref[...]
        o_ref[...] = x_ref[...] + _mod_slice(mod_ref, gate_idx) * acc

    @pl.when(i < n_prompt_tiles)
    def _():
        finish(yp_ref, xp_ref)

    @pl.when(i >= n_prompt_tiles)
    def _():
        finish(ys_ref, xs_ref)


def _out_proj(y_prompt, y_sample, w, b, x, mod, gate_idx, name):
    k = w.shape[0]
    npt = PROMPT_ROWS // TM
    prompt_map = lambda i: (jnp.minimum(i, npt - 1), 0)
    sample_map = lambda i: (jnp.maximum(i - npt, 0), 0)
    if isinstance(x, tuple):
        x_args = list(x)
        x_specs = [pl.BlockSpec((TM, D), prompt_map), pl.BlockSpec((TM, D), sample_map)]
    else:
        x_args = [x]
        x_specs = [pl.BlockSpec((TM, D), lambda i: (i, 0))]
    return pl.pallas_call(
        functools.partial(_out_kernel, gate_idx=gate_idx, n_prompt_tiles=npt),
        out_shape=jax.ShapeDtypeStruct((N_ROWS, D), F32),
        grid=(N_ROWS // TM,),
        in_specs=[
            pl.BlockSpec((TM, k), prompt_map),
            pl.BlockSpec((TM, k), sample_map),
            pl.BlockSpec((k, D), lambda i: (0, 0)),
            pl.BlockSpec((1, D), lambda i: (0, 0)),
        ] + x_specs + [
            pl.BlockSpec((None, 1, 6 * D), lambda i: (i // TILES_PER_GROUP, 0, 0)),
        ],
        out_specs=pl.BlockSpec((TM, D), lambda i: (i, 0)),
        compiler_params=_cparams(("parallel",)),
        name=name,
    )(y_prompt, y_sample, w, b.reshape(1, D), *x_args, mod)


def _mlp_kernel(x_ref, mod_ref, g_ref, w1_ref, w2_ref, fg_ref, o_ref, *, tf, final_norm):
    x = x_ref[...]
    h = _norm_mod(x, g_ref[...], _mod_slice(mod_ref, 3), _mod_slice(mod_ref, 4)).astype(BF16)
    acc = None
    for f0 in range(0, D_FF, tf):
        hid = jnp.maximum(jnp.dot(h, w1_ref[:, f0:f0 + tf], preferred_element_type=F32), 0.0)
        part = jnp.dot((hid * hid).astype(BF16), w2_ref[f0:f0 + tf, :], preferred_element_type=F32)
        acc = part if acc is None else acc + part
    y = x + _mod_slice(mod_ref, 5) * acc
    if final_norm:
        ms = jnp.mean(y * y, axis=-1, keepdims=True)
        y = (y * lax.rsqrt(ms + NORM_EPS)) * fg_ref[...]
    o_ref[...] = y


def _mlp(x, mod, g, w1, w2, layer, final_g, row_tile0, n_tiles, final_norm, name):
    layer_spec = lambda r, c: pl.BlockSpec((None, r, c), lambda i: (layer, 0, 0), pipeline_mode=pl.Buffered(1))
    return pl.pallas_call(
        functools.partial(_mlp_kernel, tf=1024, final_norm=final_norm),
        out_shape=jax.ShapeDtypeStruct((n_tiles * TM, D), F32),
        grid=(n_tiles,),
        in_specs=[
            pl.BlockSpec((TM, D), lambda i: (i + row_tile0, 0)),
            pl.BlockSpec((None, 1, 6 * D), lambda i: ((i + row_tile0) // TILES_PER_GROUP, 0, 0)),
            pl.BlockSpec((1, D), lambda i: (0, 0)),
            layer_spec(D, D_FF),
            layer_spec(D_FF, D),
            pl.BlockSpec((1, D), lambda i: (0, 0)),
        ],
        out_specs=pl.BlockSpec((TM, D), lambda i: (i, 0)),
        compiler_params=_cparams(("parallel",)),
        name=name,
    )(x, mod, g, w1, w2, final_g)


def _scan_kernel(*refs, chunk, n_chunks, reverse, has_s0, has_zin, emit_state, n_cast):
    refs = list(refs)
    p_ref, q_ref, k_ref, v_ref, g_ref, gn_ref = refs[:6]
    pos = 6
    s0_ref = zin_ref = sout_ref = None
    if has_s0:
        s0_ref = refs[pos]; pos += 1
    if has_zin:
        zin_ref = refs[pos]; pos += 1
    cast_in = refs[pos:pos + n_cast]; pos += n_cast
    z_ref = refs[pos]; pos += 1
    if emit_state:
        sout_ref = refs[pos]; pos += 1
    cast_out = refs[pos:pos + n_cast]; pos += n_cast
    s_scr, d_scr, xi_scr, zeta_scr = refs[pos:pos + 4]
    for src, dst in zip(cast_in, cast_out):
        dst[...] = src[...].astype(dst.dtype)

    c = pl.program_id(1)
    log_g = [jnp.log1p(-jnp.exp2(-p_ref[h]))[:, :1] for h in range(HEADS)]

    @pl.when((c == 0) & (pl.program_id(0) == 0))
    def _():
        ri = lax.broadcasted_iota(jnp.int32, (chunk, chunk), 0)
        ci = lax.broadcasted_iota(jnp.int32, (chunk, chunk), 1)
        rel = (ci - ri) if reverse else (ri - ci)
        relf = jnp.maximum(rel, 0).astype(F32)
        t = lax.broadcasted_iota(jnp.int32, (chunk, 128), 0)
        step = ((chunk - 1) - t if reverse else t).astype(F32)
        for h in range(HEADS):
            d_scr[h] = jnp.where(rel >= 0, jnp.exp(log_g[h] * relf), 0.0)
            xi_scr[h] = jnp.exp(log_g[h] * (step + 1.0))
            zeta_scr[h] = jnp.exp(log_g[h] * ((chunk - 1.0) - step))

    @pl.when(c == 0)
    def _():
        if has_s0:
            s_scr[...] = s0_ref[...]
        else:
            s_scr[...] = jnp.zeros_like(s_scr)

    for h in range(HEADS):
        kcols = slice(h * DK, (h + 1) * DK)
        vcols = slice(h * DV, (h + 1) * DV)
        q = q_ref[:, kcols]
        k = k_ref[:, kcols]
        v = v_ref[:, vcols]
        s = s_scr[h]
        scores = lax.dot_general(q, k, (((1,), (1,)), ((), ())), preferred_element_type=F32)
        inner = jnp.dot((scores * d_scr[h]).astype(BF16), v, preferred_element_type=F32)
        cross = jnp.dot(q, s.astype(BF16), preferred_element_type=F32)
        xi = jnp.concatenate([xi_scr[h]] * (DV // 128), axis=-1)
        o = inner + cross * xi
        zeta = jnp.concatenate([zeta_scr[h]] * (DK // 128), axis=-1)
        kz = (k.astype(F32) * zeta).astype(BF16)
        upd = lax.dot_general(kz, v, (((0,), (0,)), ((), ())), preferred_element_type=F32)
        s_scr[h] = s * jnp.exp(log_g[h] * float(chunk)) + upd

        mu = jnp.mean(o, axis=-1, keepdims=True)
        dlt = o - mu
        var = jnp.mean(dlt * dlt, axis=-1, keepdims=True)
        z = _silu(g_ref[:, vcols].astype(F32)) * ((dlt * lax.rsqrt(var + GN_EPS)) * gn_ref[:, vcols])
        if has_zin:
            z = zin_ref[:, vcols] + z
        z_ref[:, vcols] = z.astype(z_ref.dtype)

    if emit_state:
        @pl.when(c == n_chunks - 1)
        def _():
            sout_ref[...] = s_scr[...]


def _scan(decay_p, proj, gn_g, s0, zin, *, row0, n_seq, seq_len, reverse, emit_state, out_dtype, name, casts=()):
    chunk = RET_CHUNK
    nc = seq_len // chunk
    rb0 = row0 // chunk

    def rb(b, c):
        cc = (nc - 1 - c) if reverse else c
        return b * nc + cc

    in_specs = [
        pl.BlockSpec((HEADS, 1, 128), lambda b, c: (0, 0, 0)),
        pl.BlockSpec((chunk, HK), lambda b, c: (rb0 + rb(b, c), 0)),
        pl.BlockSpec((chunk, HK), lambda b, c: (rb0 + rb(b, c), 1)),
        pl.BlockSpec((chunk, HV), lambda b, c: (rb0 + rb(b, c), 1)),
        pl.BlockSpec((chunk, HV), lambda b, c: (rb0 + rb(b, c), 3 if reverse else 2)),
        pl.BlockSpec((1, HV), lambda b, c: (0, 0)),
    ]
    args = [decay_p, proj, proj, proj, proj, gn_g]
    if s0 is not None:
        s0_arr, s0_layer = s0
        in_specs.append(pl.BlockSpec((None, None, HEADS, DK, DV), lambda b, c: (b, s0_layer, 0, 0, 0)))
        args.append(s0_arr)
    if zin is not None:
        in_specs.append(pl.BlockSpec((chunk, HV), lambda b, c: (rb(b, c), 0)))
        args.append(zin)
    out_shape = [jax.ShapeDtypeStruct((n_seq * seq_len, HV), out_dtype)]
    out_specs = [pl.BlockSpec((chunk, HV), lambda b, c: (rb(b, c), 0))]
    if emit_state:
        out_shape.append(jax.ShapeDtypeStruct((n_seq, HEADS, DK, DV), F32))
        out_specs.append(pl.BlockSpec((None, HEADS, DK, DV), lambda b, c: (b, 0, 0, 0)))
    for arr in casts:
        rows, cols = arr.shape
        blk = pl.BlockSpec((rows // (n_seq * nc), cols), lambda b, c: (b * nc + c, 0))
        in_specs.append(blk)
        args.append(arr)
        out_shape.append(jax.ShapeDtypeStruct(arr.shape, BF16))
        out_specs.append(blk)
    res = pl.pallas_call(
        functools.partial(_scan_kernel, chunk=chunk, n_chunks=nc, reverse=reverse, has_s0=s0 is not None,
                          has_zin=zin is not None, emit_state=emit_state, n_cast=len(casts)),
        out_shape=out_shape,
        grid=(n_seq, nc),
        in_specs=in_specs,
        out_specs=out_specs,
        scratch_shapes=[pltpu.VMEM((HEADS, DK, DV), F32), pltpu.VMEM((HEADS, chunk, chunk), F32),
                        pltpu.VMEM((HEADS, chunk, 128), F32), pltpu.VMEM((HEADS, chunk, 128), F32)],
        compiler_params=_cparams(("arbitrary", "arbitrary")),
        name=name,
    )(*args)
    n_main = 2 if emit_state else 1
    return res[0], (res[1] if emit_state else None), list(res[n_main:])


def _rope_tables():
    n = DK // 4
    inv = ROPE_BASE ** (-jnp.arange(n, dtype=F32) / n)
    t = jnp.arange(T_SAMPLE, dtype=jnp.int32)
    row = (t // GRID_W).astype(F32)
    col = (t % GRID_W).astype(F32)
    ang = jnp.concatenate([row[:, None] * inv[None, :], col[:, None] * inv[None, :]], axis=-1)
    cos = jnp.concatenate([jnp.ones((PROMPT_ROWS, DK // 2), F32)] + [jnp.cos(ang)] * N_SAMPLE_SEQ, axis=0)
    sin = jnp.concatenate([jnp.zeros((PROMPT_ROWS, DK // 2), F32)] + [jnp.sin(ang)] * N_SAMPLE_SEQ, axis=0)
    return cos, sin


def _ret_proj_kernel(xp_ref, xs_ref, mod_ref, g_ref, w_ref, cos_ref, sin_ref, o_ref, *, n_prompt_tiles):
    x = jnp.where(pl.program_id(0) < n_prompt_tiles, xp_ref[...], xs_ref[...])
    h = _norm_mod(x, g_ref[...], _mod_slice(mod_ref, 0), _mod_slice(mod_ref, 1)).astype(BF16)
    half = DK // 2
    cos, sin = cos_ref[...], sin_ref[...]
    for j in range(w_ref.shape[1] // HK):
        acc = jnp.dot(h, w_ref[:, j * HK:(j + 1) * HK], preferred_element_type=F32)
        if j < 2:
            kscale = DK ** -0.5 if j == 1 else 1.0
            for hd in range(HEADS):
                c0 = j * HK + hd * DK
                x1 = acc[:, hd * DK:hd * DK + half]
                x2 = acc[:, hd * DK + half:(hd + 1) * DK]
                o_ref[:, c0:c0 + half] = ((x1 * cos - x2 * sin) * kscale).astype(o_ref.dtype)
                o_ref[:, c0 + half:c0 + DK] = ((x2 * cos + x1 * sin) * kscale).astype(o_ref.dtype)
        else:
            o_ref[:, j * HK:(j + 1) * HK] = acc.astype(o_ref.dtype)


def _ret_proj(xp, xs, mod, g, w_in, cos, sin):
    tm = TM // 2
    npt = PROMPT_ROWS // tm
    half = DK // 2
    n_cols = w_in.shape[1]
    return pl.pallas_call(
        functools.partial(_ret_proj_kernel, n_prompt_tiles=npt),
        out_shape=jax.ShapeDtypeStruct((N_ROWS, n_cols), BF16),
        grid=(N_ROWS // tm,),
        in_specs=[
            pl.BlockSpec((tm, D), lambda i: (jnp.minimum(i, npt - 1), 0)),
            pl.BlockSpec((tm, D), lambda i: (jnp.maximum(i - npt, 0), 0)),
            pl.BlockSpec((None, 1, 6 * D), lambda i: (i // (GROUP_ROWS // tm), 0, 0)),
            pl.BlockSpec((1, D), lambda i: (0, 0)),
            _resident_spec((D, n_cols)),
            pl.BlockSpec((tm, half), lambda i: (i, 0)),
            pl.BlockSpec((tm, half), lambda i: (i, 0)),
        ],
        out_specs=pl.BlockSpec((tm, n_cols), lambda i: (i, 0)),
        compiler_params=_cparams(("parallel",)),
        name="ret_in_proj",
    )(xp, xs, mod, g, w_in, cos, sin)


def _retention_layer(xp, xs, mod, g, w_in, gn_g, w_out, decay_fwd, decay_bwd, s0_fwd, s0_bwd, casts):
    cos, sin = _rope_tables()
    proj = _ret_proj(xp, xs, mod, g, w_in.astype(BF16), cos, sin)
    gn = gn_g.reshape(1, HV)
    pf = jnp.broadcast_to(decay_fwd.astype(F32)[:, None, None], (HEADS, 1, 128))
    pb = jnp.broadcast_to(decay_bwd.astype(F32)[:, None, None], (HEADS, 1, 128))
    prompt = dict(row0=0, n_seq=N_PROMPT_SEQ, seq_len=T_PROMPT, emit_state=True)
    sample = dict(row0=PROMPT_ROWS, n_seq=N_SAMPLE_SEQ, seq_len=T_SAMPLE, emit_state=False)
    zp, sf, _ = _scan(pf, proj, gn, None, None, reverse=False, out_dtype=F32, name="ret_scan_prompt_fwd", **prompt)
    yp, sb, _ = _scan(pb, proj, gn, None, zp, reverse=True, out_dtype=BF16, name="ret_scan_prompt_bwd", **prompt)
    zs, _, cast_out = _scan(pf, proj, gn, s0_fwd, None, reverse=False, out_dtype=F32, casts=casts,
                            name="ret_scan_sample_fwd", **sample)
    ys, _, _ = _scan(pb, proj, gn, s0_bwd, zs, reverse=True, out_dtype=BF16, name="ret_scan_sample_bwd", **sample)
    x = _out_proj(yp, ys, w_out.astype(BF16), jnp.zeros((D,), F32), (xp, xs), mod, 2, "ret_out_proj")
    return x, sf, sb, cast_out


def _seq_tile_flags(i):
    n_prompt_tiles = PROMPT_ROWS // TS
    tiles_per_seq = T_SAMPLE // TS
    is_sample = i >= n_prompt_tiles
    tin = (i - n_prompt_tiles) % tiles_per_seq
    if T_PROMPT != TS:
        raise NotImplementedError("prompt sequences must be exactly one row tile")
    return is_sample & (tin > 0), is_sample & (tin < tiles_per_seq - 1)


def _conv_kernel(u_ref, up_ref, un_ref, x_ref, mod_ref, wdw_ref, bdw_ref, lng_ref, lnb_ref, w2_ref, b2_ref,
                 o_ref, ext_scr, cv_scr):
    has_prev, has_next = _seq_tile_flags(pl.program_id(0))
    n_slabs = D // 128
    for lt in range(n_slabs):
        lanes = slice(lt * 128, (lt + 1) * 128)
        ext_scr[lt, 0:CONV_HALO, :] = jnp.where(has_prev, up_ref[:, lanes], 0.0)
        ext_scr[lt, CONV_HALO:CONV_HALO + TS, :] = u_ref[:, lanes]
        ext_scr[lt, CONV_HALO + TS:, :] = jnp.where(has_next, un_ref[:, lanes], 0.0)

    rows = 32
    shift = CONV_HALO - CONV_PAD

    def body(lt, carry):
        for r0 in range(0, TS, rows):
            acc = jnp.broadcast_to(bdw_ref[lt], (rows, 128))
            for kk in range(CONV_WIDTH):
                acc = acc + wdw_ref[lt, kk:kk + 1, :] * ext_scr[lt, r0 + kk + shift:r0 + kk + shift + rows, :]
            cv_scr[lt, r0:r0 + rows, :] = acc
        return carry

    lax.fori_loop(0, n_slabs, body, 0)

    cv = jnp.concatenate([cv_scr[lt] for lt in range(n_slabs)], axis=-1)
    mu = jnp.mean(cv, axis=-1, keepdims=True)
    dlt = cv - mu
    var = jnp.mean(dlt * dlt, axis=-1, keepdims=True)
    ln = (dlt * lax.rsqrt(var + GN_EPS)) * lng_ref[...] + lnb_ref[...]
    act = _silu(ln).astype(BF16)
    y = jnp.dot(act, w2_ref[...], preferred_element_type=F32) + b2_ref[...]
    o_ref[...] = x_ref[...] + _mod_slice(mod_ref, 2) * y


def _conv_layer(x, mod, g, w_pw1, b_pw1, w_dw, b_dw, ln_g, ln_b, w_pw2, b_pw2):
    u = _glu_proj(x, mod, g, w_pw1.astype(BF16), b_pw1)
    hb = TS // CONV_HALO
    last = N_ROWS // CONV_HALO - 1
    row = lambda a: a.reshape(1, D)
    return pl.pallas_call(
        _conv_kernel,
        out_shape=jax.ShapeDtypeStruct((N_ROWS, D), F32),
        grid=(N_ROWS // TS,),
        in_specs=[
            pl.BlockSpec((TS, D), lambda i: (i, 0)),
            pl.BlockSpec((CONV_HALO, D), lambda i: (jnp.maximum(i * hb - 1, 0), 0)),
            pl.BlockSpec((CONV_HALO, D), lambda i: (jnp.minimum((i + 1) * hb, last), 0)),
            pl.BlockSpec((TS, D), lambda i: (i, 0)),
            pl.BlockSpec((None, 1, 6 * D), lambda i: (i // (GROUP_ROWS // TS), 0, 0)),
            pl.BlockSpec((D // 128, CONV_WIDTH, 128), lambda i: (0, 0, 0)),
            pl.BlockSpec((D // 128, 1, 128), lambda i: (0, 0, 0)),
            pl.BlockSpec((1, D), lambda i: (0, 0)),
            pl.BlockSpec((1, D), lambda i: (0, 0)),
            pl.BlockSpec((D, D), lambda i: (0, 0)),
            pl.BlockSpec((1, D), lambda i: (0, 0)),
        ],
        out_specs=pl.BlockSpec((TS, D), lambda i: (i, 0)),
        scratch_shapes=[pltpu.VMEM((D // 128, TS + 2 * CONV_HALO, 128), F32), pltpu.VMEM((D // 128, TS, 128), F32)],
        compiler_params=_cparams(("parallel",)),
        name="conv_dw_ln_pw2",
    )(u, u, u, x, mod, w_dw.reshape(CONV_WIDTH, D // 128, 128).transpose(1, 0, 2), b_dw.reshape(D // 128, 1, 128),
      row(ln_g), row(ln_b), w_pw2.astype(BF16), row(b_pw2))


def _window_sum(e, w):
    q, span, n = e, 1, e.shape[0]
    while 2 * span < w:
        n -= 8
        q = q[0:n] + q[span:span + n]
        span *= 2
    start = POOL_HALO - w // 2
    return q[start:start + TS] + q[start + span:start + span + TS]


def _pool_kernel(x_ref, xp_ref, xn_ref, mod_ref, g_ref, w_ref, sc_ref, o_ref, ext_scr):
    i = pl.program_id(0)
    has_prev, has_next = _seq_tile_flags(i)
    shift, scale = _mod_slice(mod_ref, 0), _mod_slice(mod_ref, 1)
    g = g_ref[...]
    x = x_ref[...]
    h = _norm_mod(x, g, shift, scale)
    ext_scr[0:POOL_HALO, :] = jnp.where(has_prev, _norm_mod(xp_ref[...], g, shift, scale), 0.0)
    ext_scr[POOL_HALO:POOL_HALO + TS, :] = h
    ext_scr[POOL_HALO + TS:2 * POOL_HALO + TS, :] = jnp.where(has_next, _norm_mod(xn_ref[...], g, shift, scale), 0.0)
    ext_scr[2 * POOL_HALO + TS:, :] = jnp.zeros((POOL_PAD, D), F32)

    seq_len = jnp.where(i >= PROMPT_ROWS // TS, T_SAMPLE, T_PROMPT)
    t = (i * TS) % seq_len + lax.broadcasted_iota(jnp.int32, (TS, GW), 0)
    outs = []
    for gi, w in enumerate(POOL_WINDOWS):
        lanes = slice(gi * GW, (gi + 1) * GW)
        tot = _window_sum(ext_scr[:, lanes], w)
        cnt = jnp.minimum(t + w // 2, seq_len) - jnp.maximum(t - w // 2, 0)
        p = tot / cnt.astype(F32) - h[:, lanes]
        outs.append(jnp.dot(p.astype(BF16), w_ref[gi], preferred_element_type=F32))
    y = jnp.concatenate(outs, axis=-1) * sc_ref[...]
    o_ref[...] = x + _mod_slice(mod_ref, 2) * y


def _pool_layer(x, mod, g, w_grp, scale):
    hb = TS // POOL_HALO
    last = N_ROWS // POOL_HALO - 1
    return pl.pallas_call(
        _pool_kernel,
        out_shape=jax.ShapeDtypeStruct((N_ROWS, D), F32),
        grid=(N_ROWS // TS,),
        in_specs=[
            pl.BlockSpec((TS, D), lambda i: (i, 0)),
            pl.BlockSpec((POOL_HALO, D), lambda i: (jnp.maximum(i * hb - 1, 0), 0)),
            pl.BlockSpec((POOL_HALO, D), lambda i: (jnp.minimum((i + 1) * hb, last), 0)),
            pl.BlockSpec((None, 1, 6 * D), lambda i: (i // (GROUP_ROWS // TS), 0, 0)),
            pl.BlockSpec((1, D), lambda i: (0, 0)),
            pl.BlockSpec((N_CGROUPS, GW, GW), lambda i: (0, 0, 0)),
            pl.BlockSpec((1, D), lambda i: (0, 0)),
        ],
        out_specs=pl.BlockSpec((TS, D), lambda i: (i, 0)),
        scratch_shapes=[pltpu.VMEM((TS + 2 * POOL_HALO + POOL_PAD, D), F32)],
        compiler_params=_cparams(("parallel",)),
        name="pool_mixer",
    )(x, x, x, mod, g, w_grp.astype(BF16), scale.reshape(1, D))


def _chan_dft_kernel(x_ref, mod_ref, g_ref, c_ref, s_ref, a_ref, b_ref):
    h = _norm_mod(x_ref[...], g_ref[...], _mod_slice(mod_ref, 0), _mod_slice(mod_ref, 1)).astype(BF16)
    c, s = c_ref[...], s_ref[...]
    for gi in range(N_CGROUPS):
        lanes = slice(gi * GW, (gi + 1) * GW)
        a_ref[:, lanes] = jnp.dot(h[:, lanes], c, preferred_element_type=F32).astype(a_ref.dtype)
        b_ref[:, lanes] = jnp.dot(h[:, lanes], s, preferred_element_type=F32).astype(b_ref.dtype)


def _dft_tables(n):
    idx = jnp.arange(n, dtype=jnp.int32)
    ang = (2.0 * np.pi / n) * ((idx[:, None] * idx[None, :]) % n).astype(F32)
    return jnp.cos(ang), jnp.sin(ang)


def _chan_dft(x, mod, g):
    c, s = _dft_tables(GW)
    return pl.pallas_call(
        _chan_dft_kernel,
        out_shape=[jax.ShapeDtypeStruct((N_ROWS, D), BF16)] * 2,
        grid=(N_ROWS // TM,),
        in_specs=[
            pl.BlockSpec((TM, D), lambda i: (i, 0)),
            pl.BlockSpec((None, 1, 6 * D), lambda i: (i // TILES_PER_GROUP, 0, 0)),
            pl.BlockSpec((1, D), lambda i: (0, 0)),
            pl.BlockSpec((GW, GW), lambda i: (0, 0)),
            pl.BlockSpec((GW, GW), lambda i: (0, 0)),
        ],
        out_specs=[pl.BlockSpec((TM, D), lambda i: (i, 0))] * 2,
        compiler_params=_cparams(("parallel",)),
        name="fnet_chan_dft",
    )(x, mod, g, c.astype(BF16), (-s).astype(BF16))


def _time_dft_kernel(c_ref, sn_ref, a_ref, b_ref, o_ref, acc_scr, *, n_k, scale):
    k = pl.program_id(2)

    @pl.when(k == 0)
    def _():
        acc_scr[...] = jnp.zeros_like(acc_scr)

    acc_scr[...] += (jnp.dot(c_ref[...], a_ref[...], preferred_element_type=F32)
                     + jnp.dot(sn_ref[...], b_ref[...], preferred_element_type=F32))

    @pl.when(k == n_k - 1)
    def _():
        o_ref[...] = (acc_scr[...] * scale).astype(o_ref.dtype)


def _time_dft(cos_t, sin_t, a, b, *, row0, n_seq, seq_len, name):
    tm = min(seq_len, 1024)
    tk = min(seq_len, 512)
    n_i, n_k = seq_len // tm, seq_len // tk
    kb0 = row0 // tk
    scale = 1.0 / math.sqrt(seq_len * GW)
    return pl.pallas_call(
        functools.partial(_time_dft_kernel, n_k=n_k, scale=scale),
        out_shape=jax.ShapeDtypeStruct((n_seq * seq_len, D), BF16),
        grid=(n_seq, n_i, n_k),
        in_specs=[
            pl.BlockSpec((tm, tk), lambda s, i, k: (i, k)),
            pl.BlockSpec((tm, tk), lambda s, i, k: (i, k)),
            pl.BlockSpec((tk, D), lambda s, i, k: (kb0 + s * n_k + k, 0)),
            pl.BlockSpec((tk, D), lambda s, i, k: (kb0 + s * n_k + k, 0)),
        ],
        out_specs=pl.BlockSpec((tm, D), lambda s, i, k: (s * n_i + i, 0)),
        scratch_shapes=[pltpu.VMEM((tm, D), F32)],
        compiler_params=_cparams(("parallel", "parallel", "arbitrary")),
        name=name,
    )(cos_t, sin_t, a, b)


def _cmul_const(xr, xi, wr, wi):
    def scaled(v, s):
        if s == 0.0:
            return None
        return v if s == 1.0 else (-v if s == -1.0 else v * s)

    def add(p, q):
        if p is None:
            return q
        return p if q is None else p + q

    return add(scaled(xr, wr), scaled(xi, -wi)), add(scaled(xi, wr), scaled(xr, wi))


def _fft_slabs(xr, xi):
    n = len(xr)
    if n == 1:
        return xr, xi
    er, ei = _fft_slabs(xr[0::2], xi[0::2])
    dr, di = _fft_slabs(xr[1::2], xi[1::2])
    out_r, out_i = [None] * n, [None] * n
    for k in range(n // 2):
        wr = float(round(math.cos(2.0 * math.pi * k / n), 15))
        wi = float(round(-math.sin(2.0 * math.pi * k / n), 15))
        tr, ti = _cmul_const(dr[k], di[k], wr, wi)
        out_r[k], out_i[k] = er[k] + tr, ei[k] + ti
        out_r[k + n // 2], out_i[k + n // 2] = er[k] - tr, ei[k] - ti
    return out_r, out_i


FFT_N1 = 8
FFT_N2 = T_SAMPLE // FFT_N1
FFT_LANES = 256
FFT_ROWS = 16


def _time_fft_kernel(a_ref, b_ref, twc_ref, tws_ref, m_ref, o_ref, z_scr, o_scr, *, scale):
    n_slabs = FFT_LANES // 128
    for lt in range(n_slabs):
        lanes = slice(lt * 128, (lt + 1) * 128)

        def body(j, carry, lanes=lanes):
            r0 = pl.multiple_of(j * FFT_ROWS, FFT_ROWS)
            xr = [a_ref[pl.ds(s * FFT_N2 + r0, FFT_ROWS), lanes].astype(F32) for s in range(FFT_N1)]
            xi = [b_ref[pl.ds(s * FFT_N2 + r0, FFT_ROWS), lanes].astype(F32) for s in range(FFT_N1)]
            yr, yi = _fft_slabs(xr, xi)
            for c in range(FFT_N1):
                if c == 0:
                    zr, zi = yr[c], yi[c]
                else:
                    tc = twc_ref[c, pl.ds(r0, FFT_ROWS), :]
                    ts = tws_ref[c, pl.ds(r0, FFT_ROWS), :]
                    zr = yr[c] * tc + yi[c] * ts
                    zi = yi[c] * tc - yr[c] * ts
                z_scr[c, pl.ds(r0, FFT_ROWS), lanes] = zr.astype(BF16)
                z_scr[c, pl.ds(FFT_N2 + r0, FFT_ROWS), lanes] = zi.astype(BF16)
            return carry

        lax.fori_loop(0, FFT_N2 // FFT_ROWS, body, 0)

    m = m_ref[...]
    for c in range(FFT_N1):
        r = jnp.dot(m, z_scr[c], preferred_element_type=F32) * scale
        for lt in range(n_slabs):
            o_scr[lt, pl.ds(c, FFT_N2, stride=FFT_N1), :] = r[:, lt * 128:(lt + 1) * 128]
    for lt in range(n_slabs):
        o_ref[:, lt * 128:(lt + 1) * 128] = o_scr[lt].astype(o_ref.dtype)


def _time_fft_sample(a, b):
    bidx = jnp.arange(FFT_N2, dtype=jnp.int32)
    cidx = jnp.arange(FFT_N1, dtype=jnp.int32)
    ang_tw = (2.0 * np.pi / T_SAMPLE) * (cidx[:, None] * bidx[None, :]).astype(F32)
    twc = jnp.broadcast_to(jnp.cos(ang_tw)[:, :, None], (FFT_N1, FFT_N2, 128))
    tws = jnp.broadcast_to(jnp.sin(ang_tw)[:, :, None], (FFT_N1, FFT_N2, 128))
    c2, s2 = _dft_tables(FFT_N2)
    m = jnp.concatenate([c2, s2], axis=1).astype(BF16)
    rb0 = PROMPT_ROWS // T_SAMPLE
    return pl.pallas_call(
        functools.partial(_time_fft_kernel, scale=1.0 / math.sqrt(T_SAMPLE * GW)),
        out_shape=jax.ShapeDtypeStruct((N_SAMPLE_SEQ * T_SAMPLE, D), BF16),
        grid=(N_SAMPLE_SEQ, D // FFT_LANES),
        in_specs=[
            pl.BlockSpec((T_SAMPLE, FFT_LANES), lambda s, l: (rb0 + s, l)),
            pl.BlockSpec((T_SAMPLE, FFT_LANES), lambda s, l: (rb0 + s, l)),
            pl.BlockSpec((FFT_N1, FFT_N2, 128), lambda s, l: (0, 0, 0)),
            pl.BlockSpec((FFT_N1, FFT_N2, 128), lambda s, l: (0, 0, 0)),
            pl.BlockSpec((FFT_N2, 2 * FFT_N2), lambda s, l: (0, 0)),
        ],
        out_specs=pl.BlockSpec((T_SAMPLE, FFT_LANES), lambda s, l: (s, l)),
        scratch_shapes=[pltpu.VMEM((FFT_N1, 2 * FFT_N2, FFT_LANES), BF16),
                        pltpu.VMEM((FFT_LANES // 128, T_SAMPLE, 128), F32)],
        compiler_params=_cparams(("parallel", "parallel")),
        name="fnet_time_fft_sample",
    )(a, b, twc, tws, m)


def _fourier_layer(x, mod, g, w, b):
    a, bn = _chan_dft(x, mod, g)
    cp, sp = _dft_tables(T_PROMPT)
    fp = _time_dft(cp.astype(BF16), sp.astype(BF16), a, bn,
                   row0=0, n_seq=N_PROMPT_SEQ, seq_len=T_PROMPT, name="fnet_time_dft_prompt")
    fs = _time_fft_sample(a, bn)
    return _out_proj(fp, fs, w.astype(BF16), b, x, mod, 2, "fnet_out_proj")


def kernel(x_prompt, x_sample, state_ret_fwd, state_ret_bwd, c, c_ctx, w_mod, b_mod, norm_mix_g, norm_mlp_g, mlp_w1, mlp_w2, ret_w_in, ret_gn_g, ret_w_out, ret_decay_fwd, ret_decay_bwd, conv_w_pw1, conv_b_pw1, conv_w_dw, conv_b_dw, conv_ln_g, conv_ln_b, conv_w_pw2, conv_b_pw2, pool_w, pool_scale, fnet_w, fnet_b, final_norm_g):
    depth = w_mod.shape[0]
    x = (x_prompt.reshape(PROMPT_ROWS, D), x_sample.reshape(N_SAMPLE_SEQ * T_SAMPLE, D))
    cond = jnp.concatenate([c_ctx[None, :], c, jnp.zeros((MOD_ROWS - N_GROUPS, D), F32)], axis=0)
    mod_all = _adaln_all(cond, w_mod, b_mod).reshape(depth, MOD_ROWS, 1, 6 * D)
    final_g = final_norm_g.reshape(1, D)
    w1_all = w2_all = None
    new_f, new_b = [], []
    y_prompt = y_sample = None
    for i in range(depth):
        kind, j = i % 4, i // 4
        mod = mod_all[i]
        g_mix = norm_mix_g[i].reshape(1, D)
        if kind != 0 and isinstance(x, tuple):
            x = jnp.concatenate(x, axis=0)
        if kind == 0:
            xp, xs = x if isinstance(x, tuple) else (x[:PROMPT_ROWS], x[PROMPT_ROWS:])
            casts = [] if w1_all is not None else [mlp_w1.reshape(depth * D, D_FF), mlp_w2.reshape(depth * D_FF, D)]
            x, sf, sb, cast_out = _retention_layer(xp, xs, mod, g_mix, ret_w_in[j], ret_gn_g[j], ret_w_out[j],
                                                   ret_decay_fwd[j], ret_decay_bwd[j],
                                                   (state_ret_fwd, j), (state_ret_bwd, j), casts)
            if cast_out:
                w1_all, w2_all = cast_out[0].reshape(mlp_w1.shape), cast_out[1].reshape(mlp_w2.shape)
            new_f.append(sf)
            new_b.append(sb)
        elif kind == 1:
            x = _conv_layer(x, mod, g_mix, conv_w_pw1[j], conv_b_pw1[j], conv_w_dw[j], conv_b_dw[j],
                            conv_ln_g[j], conv_ln_b[j], conv_w_pw2[j], conv_b_pw2[j])
        elif kind == 2:
            x = _pool_layer(x, mod, g_mix, pool_w[j], pool_scale[j])
        else:
            x = _fourier_layer(x, mod, g_mix, fnet_w[j], fnet_b[j])
        g_mlp = norm_mlp_g[i].reshape(1, D)
        if w1_all is None:
            w1_all, w2_all = mlp_w1.astype(BF16), mlp_w2.astype(BF16)
        if i == depth - 1:
            npt = PROMPT_ROWS // TM
            y_prompt = _mlp(x, mod, g_mlp, w1_all, w2_all, i, final_g, 0, npt, True, "mlp_final_prompt")
            y_sample = _mlp(x, mod, g_mlp, w1_all, w2_all, i, final_g, npt, N_ROWS // TM - npt, True,
                            "mlp_final_sample")
        else:
            x = _mlp(x, mod, g_mlp, w1_all, w2_all, i, final_g, 0, N_ROWS // TM, False, "mlp")
    return (y_prompt.reshape(N_PROMPT_SEQ, T_PROMPT, D),
            y_sample.reshape(N_SAMPLE_SEQ, T_SAMPLE, D),
            jnp.stack(new_f, axis=1),
            jnp.stack(new_b, axis=1))
```

```python
import functools
import math

import numpy as np
import jax
import jax.numpy as jnp
from jax import lax
from jax.experimental import pallas as pl
from jax.experimental.pallas import tpu as pltpu

F32 = jnp.float32
BF16 = jnp.bfloat16

D = 1024
D_FF = 4 * D
N_PROMPT_SEQ = 16
T_PROMPT = 256
N_SAMPLE_SEQ = 4
T_SAMPLE = 4096
GROUP_ROWS = 4096
N_GROUPS = 1 + N_SAMPLE_SEQ
N_ROWS = N_GROUPS * GROUP_ROWS
PROMPT_ROWS = N_PROMPT_SEQ * T_PROMPT
MOD_ROWS = 16
GRID_W = 64
HEADS = 4
DK = 256
DV = 512
HK = HEADS * DK
HV = HEADS * DV
RET_CHUNK = 256
ROPE_BASE = 10000.0
CONV_WIDTH = 31
CONV_PAD = CONV_WIDTH // 2
CONV_HALO = 16
POOL_WINDOWS = (2, 4, 8, 16)
POOL_HALO = 8
POOL_PAD = 16
GW = 256
N_CGROUPS = D // GW
NORM_EPS = 1e-6
GN_EPS = 1e-5

TM = 1024
TILES_PER_GROUP = GROUP_ROWS // TM
TS = 256
VMEM_LIMIT = 56 * 1024 * 1024


def _cparams(sem):
    return pltpu.CompilerParams(dimension_semantics=sem, vmem_limit_bytes=VMEM_LIMIT)


def _norm_mod(x, g, shift, scale):
    ms = jnp.mean(x * x, axis=-1, keepdims=True)
    y = (x * lax.rsqrt(ms + NORM_EPS)) * g
    return y * (1.0 + scale) + shift


def _mod_slice(mod_ref, idx):
    return mod_ref[:, idx * D:(idx + 1) * D]


def _silu(x):
    return x * jax.nn.sigmoid(x)


def _mod_kernel(cond_ref, w_ref, b_ref, o_ref):
    a = _silu(cond_ref[...]).astype(BF16)
    o_ref[...] = jnp.dot(a, w_ref[...].astype(BF16), preferred_element_type=F32) + b_ref[...]


def _adaln_all(cond, w_mod, b_mod):
    depth = w_mod.shape[0]
    tn = 1536
    return pl.pallas_call(
        _mod_kernel,
        out_shape=jax.ShapeDtypeStruct((depth, MOD_ROWS, 6 * D), F32),
        grid=(depth, 6 * D // tn),
        in_specs=[
            pl.BlockSpec((MOD_ROWS, D), lambda l, j: (0, 0)),
            pl.BlockSpec((None, D, tn), lambda l, j: (l, 0, j)),
            pl.BlockSpec((None, 1, tn), lambda l, j: (l, 0, j)),
        ],
        out_specs=pl.BlockSpec((None, MOD_ROWS, tn), lambda l, j: (l, 0, j)),
        compiler_params=_cparams(("parallel", "parallel")),
        name="adaln_params",
    )(cond, w_mod, b_mod.reshape(depth, 1, 6 * D))


def _resident_spec(shape):
    return pl.BlockSpec(shape, lambda *_: (0,) * len(shape), pipeline_mode=pl.Buffered(1))


def _glu_kernel(x_ref, mod_ref, g_ref, w_ref, b_ref, o_ref):
    h = _norm_mod(x_ref[...], g_ref[...], _mod_slice(mod_ref, 0), _mod_slice(mod_ref, 1)).astype(BF16)
    a = jnp.dot(h, w_ref[:, :D], preferred_element_type=F32) + b_ref[:, :D]
    gt = jnp.dot(h, w_ref[:, D:], preferred_element_type=F32) + b_ref[:, D:]
    o_ref[...] = a * jax.nn.sigmoid(gt)


def _glu_proj(x, mod, g, w, b):
    return pl.pallas_call(
        _glu_kernel,
        out_shape=jax.ShapeDtypeStruct((N_ROWS, D), F32),
        grid=(N_ROWS // TM,),
        in_specs=[
            pl.BlockSpec((TM, D), lambda i: (i, 0)),
            pl.BlockSpec((None, 1, 6 * D), lambda i: (i // TILES_PER_GROUP, 0, 0)),
            pl.BlockSpec((1, D), lambda i: (0, 0)),
            _resident_spec((D, 2 * D)),
            pl.BlockSpec((1, 2 * D), lambda i: (0, 0)),
        ],
        out_specs=pl.BlockSpec((TM, D), lambda i: (i, 0)),
        compiler_params=_cparams(("parallel",)),
        name="conv_pw1_glu",
    )(x, mod, g, w, b.reshape(1, 2 * D))


def _out_kernel(yp_ref, ys_ref, w_ref, b_ref, xp_ref, *rest, gate_idx, n_prompt_tiles):
    xs_ref = rest[0] if len(rest) == 3 else xp_ref
    mod_ref, o_ref = rest[-2:]
    i = pl.program_id(0)
    w = w_ref[...]

    def finish(y_ref, x_ref):
        acc = jnp.dot(y_ref[...], w, preferred_element_type=F32) + b_ref[...]
        o_ref[...] = x_ref[...] + _mod_slice(mod_ref, gate_idx) * acc

    @pl.when(i < n_prompt_tiles)
    def _():
        finish(yp_ref, xp_ref)

    @pl.when(i >= n_prompt_tiles)
    def _():
        finish(ys_ref, xs_ref)


def _out_proj(y_prompt, y_sample, w, b, x, mod, gate_idx, name):
    k = w.shape[0]
    npt = PROMPT_ROWS // TM
    prompt_map = lambda i: (jnp.minimum(i, npt - 1), 0)
    sample_map = lambda i: (jnp.maximum(i - npt, 0), 0)
    if isinstance(x, tuple):
        x_args = list(x)
        x_specs = [pl.BlockSpec((TM, D), prompt_map), pl.BlockSpec((TM, D), sample_map)]
    else:
        x_args = [x]
        x_specs = [pl.BlockSpec((TM, D), lambda i: (i, 0))]
    return pl.pallas_call(
        functools.partial(_out_kernel, gate_idx=gate_idx, n_prompt_tiles=npt),
        out_shape=jax.ShapeDtypeStruct((N_ROWS, D), F32),
        grid=(N_ROWS // TM,),
        in_specs=[
            pl.BlockSpec((TM, k), prompt_map),
            pl.BlockSpec((TM, k), sample_map),
            pl.BlockSpec((k, D), lambda i: (0, 0)),
            pl.BlockSpec((1, D), lambda i: (0, 0)),
        ] + x_specs + [
            pl.BlockSpec((None, 1, 6 * D), lambda i: (i // TILES_PER_GROUP, 0, 0)),
        ],
        out_specs=pl.BlockSpec((TM, D), lambda i: (i, 0)),
        compiler_params=_cparams(("parallel",)),
        name=name,
    )(y_prompt, y_sample, w, b.reshape(1, D), *x_args, mod)


def _mlp_kernel(x_ref, mod_ref, g_ref, w1_ref, w2_ref, fg_ref, o_ref, *, tf, final_norm):
    x = x_ref[...]
    h = _norm_mod(x, g_ref[...], _mod_slice(mod_ref, 3), _mod_slice(mod_ref, 4)).astype(BF16)
    acc = None
    for f0 in range(0, D_FF, tf):
        hid = jnp.maximum(jnp.dot(h, w1_ref[:, f0:f0 + tf], preferred_element_type=F32), 0.0)
        part = jnp.dot((hid * hid).astype(BF16), w2_ref[f0:f0 + tf, :], preferred_element_type=F32)
        acc = part if acc is None else acc + part
    y = x + _mod_slice(mod_ref, 5) * acc
    if final_norm:
        ms = jnp.mean(y * y, axis=-1, keepdims=True)
        y = (y * lax.rsqrt(ms + NORM_EPS)) * fg_ref[...]
    o_ref[...] = y


def _mlp(x, mod, g, w1, w2, layer, final_g, row_tile0, n_tiles, final_norm, name):
    layer_spec = lambda r, c: pl.BlockSpec((None, r, c), lambda i: (layer, 0, 0), pipeline_mode=pl.Buffered(1))
    return pl.pallas_call(
        functools.partial(_mlp_kernel, tf=1024, final_norm=final_norm),
        out_shape=jax.ShapeDtypeStruct((n_tiles * TM, D), F32),
        grid=(n_tiles,),
        in_specs=[
            pl.BlockSpec((TM, D), lambda i: (i + row_tile0, 0)),
            pl.BlockSpec((None, 1, 6 * D), lambda i: ((i + row_tile0) // TILES_PER_GROUP, 0, 0)),
            pl.BlockSpec((1, D), lambda i: (0, 0)),
            layer_spec(D, D_FF),
            layer_spec(D_FF, D),
            pl.BlockSpec((1, D), lambda i: (0, 0)),
        ],
        out_specs=pl.BlockSpec((TM, D), lambda i: (i, 0)),
        compiler_params=_cparams(("parallel",)),
        name=name,
    )(x, mod, g, w1, w2, final_g)


def _scan_kernel(*refs, chunk, n_chunks, reverse, has_s0, has_zin, emit_state, n_cast):
    refs = list(refs)
    p_ref, q_ref, k_ref, v_ref, g_ref, gn_ref = refs[:6]
    pos = 6
    s0_ref = zin_ref = sout_ref = None
    if has_s0:
        s0_ref = refs[pos]; pos += 1
    if has_zin:
        zin_ref = refs[pos]; pos += 1
    cast_in = refs[pos:pos + n_cast]; pos += n_cast
    z_ref = refs[pos]; pos += 1
    if emit_state:
        sout_ref = refs[pos]; pos += 1
    cast_out = refs[pos:pos + n_cast]; pos += n_cast
    s_scr, d_scr, xi_scr, zeta_scr = refs[pos:pos + 4]
    for src, dst in zip(cast_in, cast_out):
        dst[...] = src[...].astype(dst.dtype)

    c = pl.program_id(1)
    log_g = [jnp.log1p(-jnp.exp2(-p_ref[h]))[:, :1] for h in range(HEADS)]

    @pl.when((c == 0) & (pl.program_id(0) == 0))
    def _():
        ri = lax.broadcasted_iota(jnp.int32, (chunk, chunk), 0)
        ci = lax.broadcasted_iota(jnp.int32, (chunk, chunk), 1)
        rel = (ci - ri) if reverse else (ri - ci)
        relf = jnp.maximum(rel, 0).astype(F32)
        t = lax.broadcasted_iota(jnp.int32, (chunk, 128), 0)
        step = ((chunk - 1) - t if reverse else t).astype(F32)
        for h in range(HEADS):
            d_scr[h] = jnp.where(rel >= 0, jnp.exp(log_g[h] * relf), 0.0)
            xi_scr[h] = jnp.exp(log_g[h] * (step + 1.0))
            zeta_scr[h] = jnp.exp(log_g[h] * ((chunk - 1.0) - step))

    @pl.when(c == 0)
    def _():
        if has_s0:
            s_scr[...] = s0_ref[...]
        else:
            s_scr[...] = jnp.zeros_like(s_scr)

    for h in range(HEADS):
        kcols = slice(h * DK, (h + 1) * DK)
        vcols = slice(h * DV, (h + 1) * DV)
        q = q_ref[:, kcols]
        k = k_ref[:, kcols]
        v = v_ref[:, vcols]
        s = s_scr[h]
        scores = lax.dot_general(q, k, (((1,), (1,)), ((), ())), preferred_element_type=F32)
        inner = jnp.dot((scores * d_scr[h]).astype(BF16), v, preferred_element_type=F32)
        cross = jnp.dot(q, s.astype(BF16), preferred_element_type=F32)
        xi = jnp.concatenate([xi_scr[h]] * (DV // 128), axis=-1)
        o = inner + cross * xi
        zeta = jnp.concatenate([zeta_scr[h]] * (DK // 128), axis=-1)
        kz = (k.astype(F32) * zeta).astype(BF16)
        upd = lax.dot_general(kz, v, (((0,), (0,)), ((), ())), preferred_element_type=F32)
        s_scr[h] = s * jnp.exp(log_g[h] * float(chunk)) + upd

        mu = jnp.mean(o, axis=-1, keepdims=True)
        dlt = o - mu
        var = jnp.mean(dlt * dlt, axis=-1, keepdims=True)
        z = _silu(g_ref[:, vcols].astype(F32)) * ((dlt * lax.rsqrt(var + GN_EPS)) * gn_ref[:, vcols])
        if has_zin:
            z = zin_ref[:, vcols].astype(F32) + z
        z_ref[:, vcols] = z.astype(z_ref.dtype)

    if emit_state:
        @pl.when(c == n_chunks - 1)
        def _():
            sout_ref[...] = s_scr[...]


def _scan(decay_p, proj, gn_g, s0, zin, *, row0, n_seq, seq_len, reverse, emit_state, out_dtype, name, casts=()):
    chunk = RET_CHUNK
    nc = seq_len // chunk
    rb0 = row0 // chunk

    def rb(b, c):
        cc = (nc - 1 - c) if reverse else c
        return b * nc + cc

    in_specs = [
        pl.BlockSpec((HEADS, 1, 128), lambda b, c: (0, 0, 0)),
        pl.BlockSpec((chunk, HK), lambda b, c: (rb0 + rb(b, c), 0)),
        pl.BlockSpec((chunk, HK), lambda b, c: (rb0 + rb(b, c), 1)),
        pl.BlockSpec((chunk, HV), lambda b, c: (rb0 + rb(b, c), 1)),
        pl.BlockSpec((chunk, HV), lambda b, c: (rb0 + rb(b, c), 3 if reverse else 2)),
        pl.BlockSpec((1, HV), lambda b, c: (0, 0)),
    ]
    args = [decay_p, proj, proj, proj, proj, gn_g]
    if s0 is not None:
        s0_arr, s0_layer = s0
        in_specs.append(pl.BlockSpec((None, None, HEADS, DK, DV), lambda b, c: (b, s0_layer, 0, 0, 0)))
        args.append(s0_arr)
    if zin is not None:
        in_specs.append(pl.BlockSpec((chunk, HV), lambda b, c: (rb(b, c), 0)))
        args.append(zin)
    out_shape = [jax.ShapeDtypeStruct((n_seq * seq_len, HV), out_dtype)]
    out_specs = [pl.BlockSpec((chunk, HV), lambda b, c: (rb(b, c), 0))]
    if emit_state:
        out_shape.append(jax.ShapeDtypeStruct((n_seq, HEADS, DK, DV), F32))
        out_specs.append(pl.BlockSpec((None, HEADS, DK, DV), lambda b, c: (b, 0, 0, 0)))
    for arr in casts:
        rows, cols = arr.shape
        blk = pl.BlockSpec((rows // (n_seq * nc), cols), lambda b, c: (b * nc + c, 0))
        in_specs.append(blk)
        args.append(arr)
        out_shape.append(jax.ShapeDtypeStruct(arr.shape, BF16))
        out_specs.append(blk)
    res = pl.pallas_call(
        functools.partial(_scan_kernel, chunk=chunk, n_chunks=nc, reverse=reverse, has_s0=s0 is not None,
                          has_zin=zin is not None, emit_state=emit_state, n_cast=len(casts)),
        out_shape=out_shape,
        grid=(n_seq, nc),
        in_specs=in_specs,
        out_specs=out_specs,
        scratch_shapes=[pltpu.VMEM((HEADS, DK, DV), F32), pltpu.VMEM((HEADS, chunk, chunk), F32),
                        pltpu.VMEM((HEADS, chunk, 128), F32), pltpu.VMEM((HEADS, chunk, 128), F32)],
        compiler_params=_cparams(("arbitrary", "arbitrary")),
        name=name,
    )(*args)
    n_main = 2 if emit_state else 1
    return res[0], (res[1] if emit_state else None), list(res[n_main:])


def _rope_tables():
    n = DK // 4
    inv = ROPE_BASE ** (-jnp.arange(n, dtype=F32) / n)
    t = jnp.arange(T_SAMPLE, dtype=jnp.int32)
    row = (t // GRID_W).astype(F32)
    col = (t % GRID_W).astype(F32)
    ang = jnp.concatenate([row[:, None] * inv[None, :], col[:, None] * inv[None, :]], axis=-1)
    cos = jnp.concatenate([jnp.ones((PROMPT_ROWS, DK // 2), F32)] + [jnp.cos(ang)] * N_SAMPLE_SEQ, axis=0)
    sin = jnp.concatenate([jnp.zeros((PROMPT_ROWS, DK // 2), F32)] + [jnp.sin(ang)] * N_SAMPLE_SEQ, axis=0)
    return cos, sin


def _ret_proj_kernel(xp_ref, xs_ref, mod_ref, g_ref, w_ref, cos_ref, sin_ref, o_ref, *, n_prompt_tiles):
    x = jnp.where(pl.program_id(0) < n_prompt_tiles, xp_ref[...], xs_ref[...])
    h = _norm_mod(x, g_ref[...], _mod_slice(mod_ref, 0), _mod_slice(mod_ref, 1)).astype(BF16)
    half = DK // 2
    cos, sin = cos_ref[...], sin_ref[...]
    for j in range(w_ref.shape[1] // HK):
        acc = jnp.dot(h, w_ref[:, j * HK:(j + 1) * HK], preferred_element_type=F32)
        if j < 2:
            kscale = DK ** -0.5 if j == 1 else 1.0
            for hd in range(HEADS):
                c0 = j * HK + hd * DK
                x1 = acc[:, hd * DK:hd * DK + half]
                x2 = acc[:, hd * DK + half:(hd + 1) * DK]
                o_ref[:, c0:c0 + half] = ((x1 * cos - x2 * sin) * kscale).astype(o_ref.dtype)
                o_ref[:, c0 + half:c0 + DK] = ((x2 * cos + x1 * sin) * kscale).astype(o_ref.dtype)
        else:
            o_ref[:, j * HK:(j + 1) * HK] = acc.astype(o_ref.dtype)


def _ret_proj(xp, xs, mod, g, w_in, cos, sin):
    tm = TM // 2
    npt = PROMPT_ROWS // tm
    half = DK // 2
    n_cols = w_in.shape[1]
    return pl.pallas_call(
        functools.partial(_ret_proj_kernel, n_prompt_tiles=npt),
        out_shape=jax.ShapeDtypeStruct((N_ROWS, n_cols), BF16),
        grid=(N_ROWS // tm,),
        in_specs=[
            pl.BlockSpec((tm, D), lambda i: (jnp.minimum(i, npt - 1), 0)),
            pl.BlockSpec((tm, D), lambda i: (jnp.maximum(i - npt, 0), 0)),
            pl.BlockSpec((None, 1, 6 * D), lambda i: (i // (GROUP_ROWS // tm), 0, 0)),
            pl.BlockSpec((1, D), lambda i: (0, 0)),
            _resident_spec((D, n_cols)),
            pl.BlockSpec((tm, half), lambda i: (i, 0)),
            pl.BlockSpec((tm, half), lambda i: (i, 0)),
        ],
        out_specs=pl.BlockSpec((tm, n_cols), lambda i: (i, 0)),
        compiler_params=_cparams(("parallel",)),
        name="ret_in_proj",
    )(xp, xs, mod, g, w_in, cos, sin)


def _retention_layer(xp, xs, mod, g, w_in, gn_g, w_out, decay_fwd, decay_bwd, s0_fwd, s0_bwd, casts):
    cos, sin = _rope_tables()
    proj = _ret_proj(xp, xs, mod, g, w_in.astype(BF16), cos, sin)
    gn = gn_g.reshape(1, HV)
    pf = jnp.broadcast_to(decay_fwd.astype(F32)[:, None, None], (HEADS, 1, 128))
    pb = jnp.broadcast_to(decay_bwd.astype(F32)[:, None, None], (HEADS, 1, 128))
    prompt = dict(row0=0, n_seq=N_PROMPT_SEQ, seq_len=T_PROMPT, emit_state=True)
    sample = dict(row0=PROMPT_ROWS, n_seq=N_SAMPLE_SEQ, seq_len=T_SAMPLE, emit_state=False)
    zp, sf, _ = _scan(pf, proj, gn, None, None, reverse=False, out_dtype=BF16, name="ret_scan_prompt_fwd", **prompt)
    yp, sb, _ = _scan(pb, proj, gn, None, zp, reverse=True, out_dtype=BF16, name="ret_scan_prompt_bwd", **prompt)
    zs, _, cast_out = _scan(pf, proj, gn, s0_fwd, None, reverse=False, out_dtype=BF16, casts=casts,
                            name="ret_scan_sample_fwd", **sample)
    ys, _, _ = _scan(pb, proj, gn, s0_bwd, zs, reverse=True, out_dtype=BF16, name="ret_scan_sample_bwd", **sample)
    x = _out_proj(yp, ys, w_out.astype(BF16), jnp.zeros((D,), F32), (xp, xs), mod, 2, "ret_out_proj")
    return x, sf, sb, cast_out


def _seq_tile_flags(i):
    n_prompt_tiles = PROMPT_ROWS // TS
    tiles_per_seq = T_SAMPLE // TS
    is_sample = i >= n_prompt_tiles
    tin = (i - n_prompt_tiles) % tiles_per_seq
    if T_PROMPT != TS:
        raise NotImplementedError("prompt sequences must be exactly one row tile")
    return is_sample & (tin > 0), is_sample & (tin < tiles_per_seq - 1)


def _conv_kernel(u_ref, up_ref, un_ref, x_ref, mod_ref, wdw_ref, bdw_ref, lng_ref, lnb_ref, w2_ref, b2_ref,
                 o_ref, ext_scr, cv_scr):
    has_prev, has_next = _seq_tile_flags(pl.program_id(0))
    n_slabs = D // 128
    for lt in range(n_slabs):
        lanes = slice(lt * 128, (lt + 1) * 128)
        ext_scr[lt, 0:CONV_HALO, :] = jnp.where(has_prev, up_ref[:, lanes], 0.0)
        ext_scr[lt, CONV_HALO:CONV_HALO + TS, :] = u_ref[:, lanes]
        ext_scr[lt, CONV_HALO + TS:, :] = jnp.where(has_next, un_ref[:, lanes], 0.0)

    rows = 32
    shift = CONV_HALO - CONV_PAD

    def body(lt, carry):
        for r0 in range(0, TS, rows):
            acc = jnp.broadcast_to(bdw_ref[lt], (rows, 128))
            for kk in range(CONV_WIDTH):
                acc = acc + wdw_ref[lt, kk:kk + 1, :] * ext_scr[lt, r0 + kk + shift:r0 + kk + shift + rows, :]
            cv_scr[lt, r0:r0 + rows, :] = acc
        return carry

    lax.fori_loop(0, n_slabs, body, 0)

    cv = jnp.concatenate([cv_scr[lt] for lt in range(n_slabs)], axis=-1)
    mu = jnp.mean(cv, axis=-1, keepdims=True)
    dlt = cv - mu
    var = jnp.mean(dlt * dlt, axis=-1, keepdims=True)
    ln = (dlt * lax.rsqrt(var + GN_EPS)) * lng_ref[...] + lnb_ref[...]
    act = _silu(ln).astype(BF16)
    y = jnp.dot(act, w2_ref[...], preferred_element_type=F32) + b2_ref[...]
    o_ref[...] = x_ref[...] + _mod_slice(mod_ref, 2) * y


def _conv_layer(x, mod, g, w_pw1, b_pw1, w_dw, b_dw, ln_g, ln_b, w_pw2, b_pw2):
    u = _glu_proj(x, mod, g, w_pw1.astype(BF16), b_pw1)
    hb = TS // CONV_HALO
    last = N_ROWS // CONV_HALO - 1
    row = lambda a: a.reshape(1, D)
    return pl.pallas_call(
        _conv_kernel,
        out_shape=jax.ShapeDtypeStruct((N_ROWS, D), F32),
        grid=(N_ROWS // TS,),
        in_specs=[
            pl.BlockSpec((TS, D), lambda i: (i, 0)),
            pl.BlockSpec((CONV_HALO, D), lambda i: (jnp.maximum(i * hb - 1, 0), 0)),
            pl.BlockSpec((CONV_HALO, D), lambda i: (jnp.minimum((i + 1) * hb, last), 0)),
            pl.BlockSpec((TS, D), lambda i: (i, 0)),
            pl.BlockSpec((None, 1, 6 * D), lambda i: (i // (GROUP_ROWS // TS), 0, 0)),
            pl.BlockSpec((D // 128, CONV_WIDTH, 128), lambda i: (0, 0, 0)),
            pl.BlockSpec((D // 128, 1, 128), lambda i: (0, 0, 0)),
            pl.BlockSpec((1, D), lambda i: (0, 0)),
            pl.BlockSpec((1, D), lambda i: (0, 0)),
            pl.BlockSpec((D, D), lambda i: (0, 0)),
            pl.BlockSpec((1, D), lambda i: (0, 0)),
        ],
        out_specs=pl.BlockSpec((TS, D), lambda i: (i, 0)),
        scratch_shapes=[pltpu.VMEM((D // 128, TS + 2 * CONV_HALO, 128), F32), pltpu.VMEM((D // 128, TS, 128), F32)],
        compiler_params=_cparams(("parallel",)),
        name="conv_dw_ln_pw2",
    )(u, u, u, x, mod, w_dw.reshape(CONV_WIDTH, D // 128, 128).transpose(1, 0, 2), b_dw.reshape(D // 128, 1, 128),
      row(ln_g), row(ln_b), w_pw2.astype(BF16), row(b_pw2))


def _window_sum(e, w):
    q, span, n = e, 1, e.shape[0]
    while 2 * span < w:
        n -= 8
        q = q[0:n] + q[span:span + n]
        span *= 2
    start = POOL_HALO - w // 2
    return q[start:start + TS] + q[start + span:start + span + TS]


def _pool_kernel(x_ref, xp_ref, xn_ref, mod_ref, g_ref, w_ref, sc_ref, o_ref, ext_scr):
    i = pl.program_id(0)
    has_prev, has_next = _seq_tile_flags(i)
    shift, scale = _mod_slice(mod_ref, 0), _mod_slice(mod_ref, 1)
    g = g_ref[...]
    x = x_ref[...]
    h = _norm_mod(x, g, shift, scale)
    ext_scr[0:POOL_HALO, :] = jnp.where(has_prev, _norm_mod(xp_ref[...], g, shift, scale), 0.0)
    ext_scr[POOL_HALO:POOL_HALO + TS, :] = h
    ext_scr[POOL_HALO + TS:2 * POOL_HALO + TS, :] = jnp.where(has_next, _norm_mod(xn_ref[...], g, shift, scale), 0.0)
    ext_scr[2 * POOL_HALO + TS:, :] = jnp.zeros((POOL_PAD, D), F32)

    seq_len = jnp.where(i >= PROMPT_ROWS // TS, T_SAMPLE, T_PROMPT)
    t = (i * TS) % seq_len + lax.broadcasted_iota(jnp.int32, (TS, 128), 0)
    outs = []
    for gi, w in enumerate(POOL_WINDOWS):
        lanes = slice(gi * GW, (gi + 1) * GW)
        tot = _window_sum(ext_scr[:, lanes], w)
        cnt = jnp.minimum(t + w // 2, seq_len) - jnp.maximum(t - w // 2, 0)
        p = tot / jnp.concatenate([cnt.astype(F32)] * (GW // 128), axis=-1) - h[:, lanes]
        outs.append(jnp.dot(p.astype(BF16), w_ref[gi], preferred_element_type=F32))
    y = jnp.concatenate(outs, axis=-1) * sc_ref[...]
    o_ref[...] = x + _mod_slice(mod_ref, 2) * y


def _pool_layer(x, mod, g, w_grp, scale):
    hb = TS // POOL_HALO
    last = N_ROWS // POOL_HALO - 1
    return pl.pallas_call(
        _pool_kernel,
        out_shape=jax.ShapeDtypeStruct((N_ROWS, D), F32),
        grid=(N_ROWS // TS,),
        in_specs=[
            pl.BlockSpec((TS, D), lambda i: (i, 0)),
            pl.BlockSpec((POOL_HALO, D), lambda i: (jnp.maximum(i * hb - 1, 0), 0)),
            pl.BlockSpec((POOL_HALO, D), lambda i: (jnp.minimum((i + 1) * hb, last), 0)),
            pl.BlockSpec((None, 1, 6 * D), lambda i: (i // (GROUP_ROWS // TS), 0, 0)),
            pl.BlockSpec((1, D), lambda i: (0, 0)),
            pl.BlockSpec((N_CGROUPS, GW, GW), lambda i: (0, 0, 0)),
            pl.BlockSpec((1, D), lambda i: (0, 0)),
        ],
        out_specs=pl.BlockSpec((TS, D), lambda i: (i, 0)),
        scratch_shapes=[pltpu.VMEM((TS + 2 * POOL_HALO + POOL_PAD, D), F32)],
        compiler_params=_cparams(("parallel",)),
        name="pool_mixer",
    )(x, x, x, mod, g, w_grp.astype(BF16), scale.reshape(1, D))


def _chan_dft_kernel(x_ref, mod_ref, g_ref, c_ref, s_ref, a_ref, b_ref):
    h = _norm_mod(x_ref[...], g_ref[...], _mod_slice(mod_ref, 0), _mod_slice(mod_ref, 1)).astype(BF16)
    c, s = c_ref[...], s_ref[...]
    for gi in range(N_CGROUPS):
        lanes = slice(gi * GW, (gi + 1) * GW)
        a_ref[:, lanes] = jnp.dot(h[:, lanes], c, preferred_element_type=F32).astype(a_ref.dtype)
        b_ref[:, lanes] = jnp.dot(h[:, lanes], s, preferred_element_type=F32).astype(b_ref.dtype)


def _dft_tables(n):
    idx = jnp.arange(n, dtype=jnp.int32)
    ang = (2.0 * np.pi / n) * ((idx[:, None] * idx[None, :]) % n).astype(F32)
    return jnp.cos(ang), jnp.sin(ang)


def _chan_dft(x, mod, g):
    c, s = _dft_tables(GW)
    return pl.pallas_call(
        _chan_dft_kernel,
        out_shape=[jax.ShapeDtypeStruct((N_ROWS, D), BF16)] * 2,
        grid=(N_ROWS // TM,),
        in_specs=[
            pl.BlockSpec((TM, D), lambda i: (i, 0)),
            pl.BlockSpec((None, 1, 6 * D), lambda i: (i // TILES_PER_GROUP, 0, 0)),
            pl.BlockSpec((1, D), lambda i: (0, 0)),
            pl.BlockSpec((GW, GW), lambda i: (0, 0)),
            pl.BlockSpec((GW, GW), lambda i: (0, 0)),
        ],
        out_specs=[pl.BlockSpec((TM, D), lambda i: (i, 0))] * 2,
        compiler_params=_cparams(("parallel",)),
        name="fnet_chan_dft",
    )(x, mod, g, c.astype(BF16), (-s).astype(BF16))


def _time_dft_kernel(c_ref, sn_ref, a_ref, b_ref, o_ref, acc_scr, *, n_k, scale):
    k = pl.program_id(2)

    @pl.when(k == 0)
    def _():
        acc_scr[...] = jnp.zeros_like(acc_scr)

    acc_scr[...] += (jnp.dot(c_ref[...], a_ref[...], preferred_element_type=F32)
                     + jnp.dot(sn_ref[...], b_ref[...], preferred_element_type=F32))

    @pl.when(k == n_k - 1)
    def _():
        o_ref[...] = (acc_scr[...] * scale).astype(o_ref.dtype)


def _time_dft(cos_t, sin_t, a, b, *, row0, n_seq, seq_len, name):
    tm = min(seq_len, 1024)
    tk = min(seq_len, 512)
    n_i, n_k = seq_len // tm, seq_len // tk
    kb0 = row0 // tk
    scale = 1.0 / math.sqrt(seq_len * GW)
    return pl.pallas_call(
        functools.partial(_time_dft_kernel, n_k=n_k, scale=scale),
        out_shape=jax.ShapeDtypeStruct((n_seq * seq_len, D), BF16),
        grid=(n_seq, n_i, n_k),
        in_specs=[
            pl.BlockSpec((tm, tk), lambda s, i, k: (i, k)),
            pl.BlockSpec((tm, tk), lambda s, i, k: (i, k)),
            pl.BlockSpec((tk, D), lambda s, i, k: (kb0 + s * n_k + k, 0)),
            pl.BlockSpec((tk, D), lambda s, i, k: (kb0 + s * n_k + k, 0)),
        ],
        out_specs=pl.BlockSpec((tm, D), lambda s, i, k: (s * n_i + i, 0)),
        scratch_shapes=[pltpu.VMEM((tm, D), F32)],
        compiler_params=_cparams(("parallel", "parallel", "arbitrary")),
        name=name,
    )(cos_t, sin_t, a, b)


def _cmul_const(xr, xi, wr, wi):
    def scaled(v, s):
        if s == 0.0:
            return None
        return v if s == 1.0 else (-v if s == -1.0 else v * s)

    def add(p, q):
        if p is None:
            return q
        return p if q is None else p + q

    return add(scaled(xr, wr), scaled(xi, -wi)), add(scaled(xi, wr), scaled(xr, wi))


def _fft_slabs(xr, xi):
    n = len(xr)
    if n == 1:
        return xr, xi
    er, ei = _fft_slabs(xr[0::2], xi[0::2])
    dr, di = _fft_slabs(xr[1::2], xi[1::2])
    out_r, out_i = [None] * n, [None] * n
    for k in range(n // 2):
        wr = float(round(math.cos(2.0 * math.pi * k / n), 15))
        wi = float(round(-math.sin(2.0 * math.pi * k / n), 15))
        tr, ti = _cmul_const(dr[k], di[k], wr, wi)
        out_r[k], out_i[k] = er[k] + tr, ei[k] + ti
        out_r[k + n // 2], out_i[k + n // 2] = er[k] - tr, ei[k] - ti
    return out_r, out_i


FFT_N1 = 8
FFT_N2 = T_SAMPLE // FFT_N1
FFT_LANES = 256
FFT_ROWS = 16


def _time_fft_kernel(a_ref, b_ref, twc_ref, tws_ref, m_ref, o_ref, z_scr, o_scr, *, scale):
    n_slabs = FFT_LANES // 128
    for lt in range(n_slabs):
        lanes = slice(lt * 128, (lt + 1) * 128)

        def body(j, carry, lanes=lanes):
            r0 = pl.multiple_of(j * FFT_ROWS, FFT_ROWS)
            xr = [a_ref[pl.ds(s * FFT_N2 + r0, FFT_ROWS), lanes].astype(F32) for s in range(FFT_N1)]
            xi = [b_ref[pl.ds(s * FFT_N2 + r0, FFT_ROWS), lanes].astype(F32) for s in range(FFT_N1)]
            yr, yi = _fft_slabs(xr, xi)
            for c in range(FFT_N1):
                if c == 0:
                    zr, zi = yr[c], yi[c]
                else:
                    tc = twc_ref[c, pl.ds(r0, FFT_ROWS), :]
                    ts = tws_ref[c, pl.ds(r0, FFT_ROWS), :]
                    zr = yr[c] * tc + yi[c] * ts
                    zi = yi[c] * tc - yr[c] * ts
                z_scr[c, pl.ds(r0, FFT_ROWS), lanes] = zr.astype(BF16)
                z_scr[c, pl.ds(FFT_N2 + r0, FFT_ROWS), lanes] = zi.astype(BF16)
            return carry

        lax.fori_loop(0, FFT_N2 // FFT_ROWS, body, 0)

    m = m_ref[...]
    for c in range(FFT_N1):
        r = jnp.dot(m, z_scr[c], preferred_element_type=F32) * scale
        for lt in range(n_slabs):
            o_scr[lt, pl.ds(c, FFT_N2, stride=FFT_N1), :] = r[:, lt * 128:(lt + 1) * 128]
    for lt in range(n_slabs):
        o_ref[:, lt * 128:(lt + 1) * 128] = o_scr[lt].astype(o_ref.dtype)


def _time_fft_sample(a, b):
    bidx = jnp.arange(FFT_N2, dtype=jnp.int32)
    cidx = jnp.arange(FFT_N1, dtype=jnp.int32)
    ang_tw = (2.0 * np.pi / T_SAMPLE) * (cidx[:, None] * bidx[None, :]).astype(F32)
    twc = jnp.broadcast_to(jnp.cos(ang_tw)[:, :, None], (FFT_N1, FFT_N2, 128))
    tws = jnp.broadcast_to(jnp.sin(ang_tw)[:, :, None], (FFT_N1, FFT_N2, 128))
    c2, s2 = _dft_tables(FFT_N2)
    m = jnp.concatenate([c2, s2], axis=1).astype(BF16)
    rb0 = PROMPT_ROWS // T_SAMPLE
    return pl.pallas_call(
        functools.partial(_time_fft_kernel, scale=1.0 / math.sqrt(T_SAMPLE * GW)),
        out_shape=jax.ShapeDtypeStruct((N_SAMPLE_SEQ * T_SAMPLE, D), BF16),
        grid=(N_SAMPLE_SEQ, D // FFT_LANES),
        in_specs=[
            pl.BlockSpec((T_SAMPLE, FFT_LANES), lambda s, l: (rb0 + s, l)),
            pl.BlockSpec((T_SAMPLE, FFT_LANES), lambda s, l: (rb0 + s, l)),
            pl.BlockSpec((FFT_N1, FFT_N2, 128), lambda s, l: (0, 0, 0)),
            pl.BlockSpec((FFT_N1, FFT_N2, 128), lambda s, l: (0, 0, 0)),
            pl.BlockSpec((FFT_N2, 2 * FFT_N2), lambda s, l: (0, 0)),
        ],
        out_specs=pl.BlockSpec((T_SAMPLE, FFT_LANES), lambda s, l: (s, l)),
        scratch_shapes=[pltpu.VMEM((FFT_N1, 2 * FFT_N2, FFT_LANES), BF16),
                        pltpu.VMEM((FFT_LANES // 128, T_SAMPLE, 128), F32)],
        compiler_params=_cparams(("parallel", "parallel")),
        name="fnet_time_fft_sample",
    )(a, b, twc, tws, m)


def _fourier_layer(x, mod, g, w, b):
    a, bn = _chan_dft(x, mod, g)
    cp, sp = _dft_tables(T_PROMPT)
    fp = _time_dft(cp.astype(BF16), sp.astype(BF16), a, bn,
                   row0=0, n_seq=N_PROMPT_SEQ, seq_len=T_PROMPT, name="fnet_time_dft_prompt")
    fs = _time_fft_sample(a, bn)
    return _out_proj(fp, fs, w.astype(BF16), b, x, mod, 2, "fnet_out_proj")


def kernel(x_prompt, x_sample, state_ret_fwd, state_ret_bwd, c, c_ctx, w_mod, b_mod, norm_mix_g, norm_mlp_g, mlp_w1, mlp_w2, ret_w_in, ret_gn_g, ret_w_out, ret_decay_fwd, ret_decay_bwd, conv_w_pw1, conv_b_pw1, conv_w_dw, conv_b_dw, conv_ln_g, conv_ln_b, conv_w_pw2, conv_b_pw2, pool_w, pool_scale, fnet_w, fnet_b, final_norm_g):
    depth = w_mod.shape[0]
    x = (x_prompt.reshape(PROMPT_ROWS, D), x_sample.reshape(N_SAMPLE_SEQ * T_SAMPLE, D))
    cond = jnp.concatenate([c_ctx[None, :], c, jnp.zeros((MOD_ROWS - N_GROUPS, D), F32)], axis=0)
    mod_all = _adaln_all(cond, w_mod, b_mod).reshape(depth, MOD_ROWS, 1, 6 * D)
    final_g = final_norm_g.reshape(1, D)
    w1_all = w2_all = None
    new_f, new_b = [], []
    y_prompt = y_sample = None
    for i in range(depth):
        kind, j = i % 4, i // 4
        mod = mod_all[i]
        g_mix = norm_mix_g[i].reshape(1, D)
        if kind != 0 and isinstance(x, tuple):
            x = jnp.concatenate(x, axis=0)
        if kind == 0:
            xp, xs = x if isinstance(x, tuple) else (x[:PROMPT_ROWS], x[PROMPT_ROWS:])
            casts = [] if w1_all is not None else [mlp_w1.reshape(depth * D, D_FF), mlp_w2.reshape(depth * D_FF, D)]
            x, sf, sb, cast_out = _retention_layer(xp, xs, mod, g_mix, ret_w_in[j], ret_gn_g[j], ret_w_out[j],
                                                   ret_decay_fwd[j], ret_decay_bwd[j],
                                                   (state_ret_fwd, j), (state_ret_bwd, j), casts)
            if cast_out:
                w1_all, w2_all = cast_out[0].reshape(mlp_w1.shape), cast_out[1].reshape(mlp_w2.shape)
            new_f.append(sf)
            new_b.append(sb)
        elif kind == 1:
            x = _conv_layer(x, mod, g_mix, conv_w_pw1[j], conv_b_pw1[j], conv_w_dw[j], conv_b_dw[j],
                            conv_ln_g[j], conv_ln_b[j], conv_w_pw2[j], conv_b_pw2[j])
        elif kind == 2:
            x = _pool_layer(x, mod, g_mix, pool_w[j], pool_scale[j])
        else:
            x = _fourier_layer(x, mod, g_mix, fnet_w[j], fnet_b[j])
        g_mlp = norm_mlp_g[i].reshape(1, D)
        if w1_all is None:
            w1_all, w2_all = mlp_w1.astype(BF16), mlp_w2.astype(BF16)
        if i == depth - 1:
            npt = PROMPT_ROWS // TM
            y_prompt = _mlp(x, mod, g_mlp, w1_all, w2_all, i, final_g, 0, npt, True, "mlp_final_prompt")
            y_sample = _mlp(x, mod, g_mlp, w1_all, w2_all, i, final_g, npt, N_ROWS // TM - npt, True,
                            "mlp_final_sample")
        else:
            x = _mlp(x, mod, g_mlp, w1_all, w2_all, i, final_g, 0, N_ROWS // TM, False, "mlp")
    return (y_prompt.reshape(N_PROMPT_SEQ, T_PROMPT, D),
            y_sample.reshape(N_SAMPLE_SEQ, T_SAMPLE, D),
            jnp.stack(new_f, axis=1),
            jnp.stack(new_b, axis=1))
```

```python
import functools
import math

import numpy as np
import jax
import jax.numpy as jnp
from jax import lax
from jax.experimental import pallas as pl
from jax.experimental.pallas import tpu as pltpu

F32 = jnp.float32
BF16 = jnp.bfloat16

D = 1024
D_FF = 4 * D
N_PROMPT_SEQ = 16
T_PROMPT = 256
N_SAMPLE_SEQ = 4
T_SAMPLE = 4096
GROUP_ROWS = 4096
N_GROUPS = 1 + N_SAMPLE_SEQ
N_ROWS = N_GROUPS * GROUP_ROWS
PROMPT_ROWS = N_PROMPT_SEQ * T_PROMPT
MOD_ROWS = 16
GRID_W = 64
HEADS = 4
DK = 256
DV = 512
HK = HEADS * DK
HV = HEADS * DV
RET_CHUNK = 256
ROPE_BASE = 10000.0
CONV_WIDTH = 31
CONV_PAD = CONV_WIDTH // 2
CONV_HALO = 16
POOL_WINDOWS = (2, 4, 8, 16)
POOL_HALO = 8
POOL_PAD = 16
GW = 256
N_CGROUPS = D // GW
NORM_EPS = 1e-6
GN_EPS = 1e-5

TM = 1024
TILES_PER_GROUP = GROUP_ROWS // TM
TS = 256
LANES = 128
SUBLANES = 8
VMEM_LIMIT = 56 * 1024 * 1024


def _cparams(sem):
    return pltpu.CompilerParams(dimension_semantics=sem, vmem_limit_bytes=VMEM_LIMIT)


def _norm_mod(x, g, shift, scale):
    ms = jnp.mean(x * x, axis=-1, keepdims=True)
    y = (x * lax.rsqrt(ms + NORM_EPS)) * g
    return y * (1.0 + scale) + shift


def _mod_slice(mod_ref, idx):
    return mod_ref[:, idx * D:(idx + 1) * D]


def _silu(x):
    return x * jax.nn.sigmoid(x)


def _mod_kernel(cond_ref, w_ref, b_ref, o_ref):
    a = _silu(cond_ref[...]).astype(BF16)
    o_ref[...] = jnp.dot(a, w_ref[...].astype(BF16), preferred_element_type=F32) + b_ref[...]


def _adaln_all(cond, w_mod, b_mod):
    depth = w_mod.shape[0]
    tn = 1536
    return pl.pallas_call(
        _mod_kernel,
        out_shape=jax.ShapeDtypeStruct((depth, MOD_ROWS, 6 * D), F32),
        grid=(depth, 6 * D // tn),
        in_specs=[
            pl.BlockSpec((MOD_ROWS, D), lambda l, j: (0, 0)),
            pl.BlockSpec((None, D, tn), lambda l, j: (l, 0, j)),
            pl.BlockSpec((None, 1, tn), lambda l, j: (l, 0, j)),
        ],
        out_specs=pl.BlockSpec((None, MOD_ROWS, tn), lambda l, j: (l, 0, j)),
        compiler_params=_cparams(("parallel", "parallel")),
        name="adaln_params",
    )(cond, w_mod, b_mod.reshape(depth, 1, 6 * D))


def _resident_spec(shape):
    return pl.BlockSpec(shape, lambda *_: (0,) * len(shape), pipeline_mode=pl.Buffered(1))


def _glu_kernel(x_ref, mod_ref, g_ref, w_ref, b_ref, o_ref):
    h = _norm_mod(x_ref[...], g_ref[...], _mod_slice(mod_ref, 0), _mod_slice(mod_ref, 1)).astype(BF16)
    a = jnp.dot(h, w_ref[:, :D], preferred_element_type=F32) + b_ref[:, :D]
    gt = jnp.dot(h, w_ref[:, D:], preferred_element_type=F32) + b_ref[:, D:]
    o_ref[...] = a * jax.nn.sigmoid(gt)


def _glu_proj(x, mod, g, w, b):
    return pl.pallas_call(
        _glu_kernel,
        out_shape=jax.ShapeDtypeStruct((N_ROWS, D), F32),
        grid=(N_ROWS // TM,),
        in_specs=[
            pl.BlockSpec((TM, D), lambda i: (i, 0)),
            pl.BlockSpec((None, 1, 6 * D), lambda i: (i // TILES_PER_GROUP, 0, 0)),
            pl.BlockSpec((1, D), lambda i: (0, 0)),
            _resident_spec((D, 2 * D)),
            pl.BlockSpec((1, 2 * D), lambda i: (0, 0)),
        ],
        out_specs=pl.BlockSpec((TM, D), lambda i: (i, 0)),
        compiler_params=_cparams(("parallel",)),
        name="conv_pw1_glu",
    )(x, mod, g, w, b.reshape(1, 2 * D))


def _out_kernel(yp_ref, ys_ref, w_ref, b_ref, xp_ref, *rest, gate_idx, n_prompt_tiles):
    xs_ref = rest[0] if len(rest) == 3 else xp_ref
    mod_ref, o_ref = rest[-2:]
    i = pl.program_id(0)
    w = w_ref[...]

    def finish(y_ref, x_ref):
        acc = jnp.dot(y_ref[...], w, preferred_element_type=F32) + b_ref[...]
        o_ref[...] = x_ref[...] + _mod_slice(mod_ref, gate_idx) * acc

    @pl.when(i < n_prompt_tiles)
    def _():
        finish(yp_ref, xp_ref)

    @pl.when(i >= n_prompt_tiles)
    def _():
        finish(ys_ref, xs_ref)


def _out_proj(y_prompt, y_sample, w, b, x, mod, gate_idx, name):
    k = w.shape[0]
    npt = PROMPT_ROWS // TM
    prompt_map = lambda i: (jnp.minimum(i, npt - 1), 0)
    sample_map = lambda i: (jnp.maximum(i - npt, 0), 0)
    if isinstance(x, tuple):
        x_args = list(x)
        x_specs = [pl.BlockSpec((TM, D), prompt_map), pl.BlockSpec((TM, D), sample_map)]
    else:
        x_args = [x]
        x_specs = [pl.BlockSpec((TM, D), lambda i: (i, 0))]
    return pl.pallas_call(
        functools.partial(_out_kernel, gate_idx=gate_idx, n_prompt_tiles=npt),
        out_shape=jax.ShapeDtypeStruct((N_ROWS, D), F32),
        grid=(N_ROWS // TM,),
        in_specs=[
            pl.BlockSpec((TM, k), prompt_map),
            pl.BlockSpec((TM, k), sample_map),
            pl.BlockSpec((k, D), lambda i: (0, 0)),
            pl.BlockSpec((1, D), lambda i: (0, 0)),
        ] + x_specs + [
            pl.BlockSpec((None, 1, 6 * D), lambda i: (i // TILES_PER_GROUP, 0, 0)),
        ],
        out_specs=pl.BlockSpec((TM, D), lambda i: (i, 0)),
        compiler_params=_cparams(("parallel",)),
        name=name,
    )(y_prompt, y_sample, w, b.reshape(1, D), *x_args, mod)


def _mlp_kernel(x_ref, mod_ref, g_ref, w1_ref, w2_ref, fg_ref, o_ref, *, tf, final_norm):
    x = x_ref[...]
    h = _norm_mod(x, g_ref[...], _mod_slice(mod_ref, 3), _mod_slice(mod_ref, 4)).astype(BF16)
    acc = None
    for f0 in range(0, D_FF, tf):
        hid = jnp.maximum(jnp.dot(h, w1_ref[:, f0:f0 + tf], preferred_element_type=F32), 0.0)
        part = jnp.dot((hid * hid).astype(BF16), w2_ref[f0:f0 + tf, :], preferred_element_type=F32)
        acc = part if acc is None else acc + part
    y = x + _mod_slice(mod_ref, 5) * acc
    if final_norm:
        ms = jnp.mean(y * y, axis=-1, keepdims=True)
        y = (y * lax.rsqrt(ms + NORM_EPS)) * fg_ref[...]
    o_ref[...] = y


def _mlp(x, mod, g, w1, w2, layer, final_g, row_tile0, n_tiles, final_norm, name):
    layer_spec = lambda r, c: pl.BlockSpec((None, r, c), lambda i: (layer, 0, 0), pipeline_mode=pl.Buffered(1))
    return pl.pallas_call(
        functools.partial(_mlp_kernel, tf=1024, final_norm=final_norm),
        out_shape=jax.ShapeDtypeStruct((n_tiles * TM, D), F32),
        grid=(n_tiles,),
        in_specs=[
            pl.BlockSpec((TM, D), lambda i: (i + row_tile0, 0)),
            pl.BlockSpec((None, 1, 6 * D), lambda i: ((i + row_tile0) // TILES_PER_GROUP, 0, 0)),
            pl.BlockSpec((1, D), lambda i: (0, 0)),
            layer_spec(D, D_FF),
            layer_spec(D_FF, D),
            pl.BlockSpec((1, D), lambda i: (0, 0)),
        ],
        out_specs=pl.BlockSpec((TM, D), lambda i: (i, 0)),
        compiler_params=_cparams(("parallel",)),
        name=name,
    )(x, mod, g, w1, w2, final_g)


def _scan_kernel(*refs, chunk, n_chunks, reverse, has_s0, has_zin, emit_state, n_cast):
    refs = list(refs)
    p_ref, q_ref, k_ref, v_ref, g_ref, gn_ref = refs[:6]
    pos = 6
    s0_ref = zin_ref = sout_ref = None
    if has_s0:
        s0_ref = refs[pos]; pos += 1
    if has_zin:
        zin_ref = refs[pos]; pos += 1
    cast_in = refs[pos:pos + n_cast]; pos += n_cast
    z_ref = refs[pos]; pos += 1
    if emit_state:
        sout_ref = refs[pos]; pos += 1
    cast_out = refs[pos:pos + n_cast]; pos += n_cast
    s_scr, d_scr, xi_scr, zeta_scr = refs[pos:pos + 4]
    for src, dst in zip(cast_in, cast_out):
        dst[...] = src[...].astype(dst.dtype)

    c = pl.program_id(1)
    log_g = [jnp.log1p(-jnp.exp2(-p_ref[h]))[:, :1] for h in range(HEADS)]

    @pl.when((c == 0) & (pl.program_id(0) == 0))
    def _():
        ri = lax.broadcasted_iota(jnp.int32, (chunk, chunk), 0)
        ci = lax.broadcasted_iota(jnp.int32, (chunk, chunk), 1)
        rel = (ci - ri) if reverse else (ri - ci)
        relf = jnp.maximum(rel, 0).astype(F32)
        t = lax.broadcasted_iota(jnp.int32, (chunk, LANES), 0)
        step = ((chunk - 1) - t if reverse else t).astype(F32)
        for h in range(HEADS):
            d_scr[h] = jnp.where(rel >= 0, jnp.exp(log_g[h] * relf), 0.0)
            xi_scr[h] = jnp.exp(log_g[h] * (step + 1.0))
            zeta_scr[h] = jnp.exp(log_g[h] * ((chunk - 1.0) - step))

    @pl.when(c == 0)
    def _():
        if has_s0:
            s_scr[...] = s0_ref[...]
        else:
            s_scr[...] = jnp.zeros_like(s_scr)

    for h in range(HEADS):
        kcols = slice(h * DK, (h + 1) * DK)
        vcols = slice(h * DV, (h + 1) * DV)
        q = q_ref[:, kcols]
        k = k_ref[:, kcols]
        v = v_ref[:, vcols]
        s = s_scr[h]
        scores = lax.dot_general(q, k, (((1,), (1,)), ((), ())), preferred_element_type=F32)
        inner = jnp.dot((scores * d_scr[h]).astype(BF16), v, preferred_element_type=F32)
        cross = jnp.dot(q, s.astype(BF16), preferred_element_type=F32)
        xi = jnp.concatenate([xi_scr[h]] * (DV // LANES), axis=-1)
        o = inner + cross * xi
        zeta = jnp.concatenate([zeta_scr[h]] * (DK // LANES), axis=-1)
        kz = (k.astype(F32) * zeta).astype(BF16)
        upd = lax.dot_general(kz, v, (((0,), (0,)), ((), ())), preferred_element_type=F32)
        s_scr[h] = s * jnp.exp(log_g[h] * float(chunk)) + upd

        mu = jnp.mean(o, axis=-1, keepdims=True)
        dlt = o - mu
        var = jnp.mean(dlt * dlt, axis=-1, keepdims=True)
        z = _silu(g_ref[:, vcols].astype(F32)) * ((dlt * lax.rsqrt(var + GN_EPS)) * gn_ref[:, vcols])
        if has_zin:
            z = zin_ref[:, vcols].astype(F32) + z
        z_ref[:, vcols] = z.astype(z_ref.dtype)

    if emit_state:
        @pl.when(c == n_chunks - 1)
        def _():
            sout_ref[...] = s_scr[...]


def _scan(decay_p, proj, gn_g, s0, zin, *, row0, n_seq, seq_len, reverse, emit_state, out_dtype, name, casts=()):
    chunk = RET_CHUNK
    nc = seq_len // chunk
    rb0 = row0 // chunk

    def rb(b, c):
        cc = (nc - 1 - c) if reverse else c
        return b * nc + cc

    in_specs = [
        pl.BlockSpec((HEADS, 1, LANES), lambda b, c: (0, 0, 0)),
        pl.BlockSpec((chunk, HK), lambda b, c: (rb0 + rb(b, c), 0)),
        pl.BlockSpec((chunk, HK), lambda b, c: (rb0 + rb(b, c), 1)),
        pl.BlockSpec((chunk, HV), lambda b, c: (rb0 + rb(b, c), 1)),
        pl.BlockSpec((chunk, HV), lambda b, c: (rb0 + rb(b, c), 3 if reverse else 2)),
        pl.BlockSpec((1, HV), lambda b, c: (0, 0)),
    ]
    args = [decay_p, proj, proj, proj, proj, gn_g]
    if s0 is not None:
        s0_arr, s0_layer = s0
        in_specs.append(pl.BlockSpec((None, None, HEADS, DK, DV), lambda b, c: (b, s0_layer, 0, 0, 0)))
        args.append(s0_arr)
    if zin is not None:
        in_specs.append(pl.BlockSpec((chunk, HV), lambda b, c: (rb(b, c), 0)))
        args.append(zin)
    out_shape = [jax.ShapeDtypeStruct((n_seq * seq_len, HV), out_dtype)]
    out_specs = [pl.BlockSpec((chunk, HV), lambda b, c: (rb(b, c), 0))]
    if emit_state:
        out_shape.append(jax.ShapeDtypeStruct((n_seq, HEADS, DK, DV), F32))
        out_specs.append(pl.BlockSpec((None, HEADS, DK, DV), lambda b, c: (b, 0, 0, 0)))
    for arr in casts:
        rows, cols = arr.shape
        blk = pl.BlockSpec((rows // (n_seq * nc), cols), lambda b, c: (b * nc + c, 0))
        in_specs.append(blk)
        args.append(arr)
        out_shape.append(jax.ShapeDtypeStruct(arr.shape, BF16))
        out_specs.append(blk)
    res = pl.pallas_call(
        functools.partial(_scan_kernel, chunk=chunk, n_chunks=nc, reverse=reverse, has_s0=s0 is not None,
                          has_zin=zin is not None, emit_state=emit_state, n_cast=len(casts)),
        out_shape=out_shape,
        grid=(n_seq, nc),
        in_specs=in_specs,
        out_specs=out_specs,
        scratch_shapes=[pltpu.VMEM((HEADS, DK, DV), F32), pltpu.VMEM((HEADS, chunk, chunk), F32),
                        pltpu.VMEM((HEADS, chunk, LANES), F32), pltpu.VMEM((HEADS, chunk, LANES), F32)],
        compiler_params=_cparams(("arbitrary", "arbitrary")),
        name=name,
    )(*args)
    n_main = 2 if emit_state else 1
    return res[0], (res[1] if emit_state else None), list(res[n_main:])


def _rope_tables():
    n = DK // 4
    inv = ROPE_BASE ** (-jnp.arange(n, dtype=F32) / n)
    t = jnp.arange(T_SAMPLE, dtype=jnp.int32)
    row = (t // GRID_W).astype(F32)
    col = (t % GRID_W).astype(F32)
    ang = jnp.concatenate([row[:, None] * inv[None, :], col[:, None] * inv[None, :]], axis=-1)
    cos = jnp.concatenate([jnp.ones((PROMPT_ROWS, DK // 2), F32)] + [jnp.cos(ang)] * N_SAMPLE_SEQ, axis=0)
    sin = jnp.concatenate([jnp.zeros((PROMPT_ROWS, DK // 2), F32)] + [jnp.sin(ang)] * N_SAMPLE_SEQ, axis=0)
    return cos, sin


def _ret_proj_kernel(xp_ref, xs_ref, mod_ref, g_ref, w_ref, cos_ref, sin_ref, o_ref, *, n_prompt_tiles):
    x = jnp.where(pl.program_id(0) < n_prompt_tiles, xp_ref[...], xs_ref[...])
    h = _norm_mod(x, g_ref[...], _mod_slice(mod_ref, 0), _mod_slice(mod_ref, 1)).astype(BF16)
    half = DK // 2
    cos, sin = cos_ref[...], sin_ref[...]
    for j in range(w_ref.shape[1] // HK):
        acc = jnp.dot(h, w_ref[:, j * HK:(j + 1) * HK], preferred_element_type=F32)
        if j < 2:
            kscale = DK ** -0.5 if j == 1 else 1.0
            for hd in range(HEADS):
                c0 = j * HK + hd * DK
                x1 = acc[:, hd * DK:hd * DK + half]
                x2 = acc[:, hd * DK + half:(hd + 1) * DK]
                o_ref[:, c0:c0 + half] = ((x1 * cos - x2 * sin) * kscale).astype(o_ref.dtype)
                o_ref[:, c0 + half:c0 + DK] = ((x2 * cos + x1 * sin) * kscale).astype(o_ref.dtype)
        else:
            o_ref[:, j * HK:(j + 1) * HK] = acc.astype(o_ref.dtype)


def _ret_proj(xp, xs, mod, g, w_in, cos, sin):
    tm = TM // 2
    npt = PROMPT_ROWS // tm
    half = DK // 2
    n_cols = w_in.shape[1]
    return pl.pallas_call(
        functools.partial(_ret_proj_kernel, n_prompt_tiles=npt),
        out_shape=jax.ShapeDtypeStruct((N_ROWS, n_cols), BF16),
        grid=(N_ROWS // tm,),
        in_specs=[
            pl.BlockSpec((tm, D), lambda i: (jnp.minimum(i, npt - 1), 0)),
            pl.BlockSpec((tm, D), lambda i: (jnp.maximum(i - npt, 0), 0)),
            pl.BlockSpec((None, 1, 6 * D), lambda i: (i // (GROUP_ROWS // tm), 0, 0)),
            pl.BlockSpec((1, D), lambda i: (0, 0)),
            _resident_spec((D, n_cols)),
            pl.BlockSpec((tm, half), lambda i: (i, 0)),
            pl.BlockSpec((tm, half), lambda i: (i, 0)),
        ],
        out_specs=pl.BlockSpec((tm, n_cols), lambda i: (i, 0)),
        compiler_params=_cparams(("parallel",)),
        name="ret_in_proj",
    )(xp, xs, mod, g, w_in, cos, sin)


def _retention_layer(xp, xs, mod, g, w_in, gn_g, w_out, decay_fwd, decay_bwd, s0_fwd, s0_bwd, casts):
    cos, sin = _rope_tables()
    proj = _ret_proj(xp, xs, mod, g, w_in.astype(BF16), cos, sin)
    gn = gn_g.reshape(1, HV)
    pf = jnp.broadcast_to(decay_fwd.astype(F32)[:, None, None], (HEADS, 1, LANES))
    pb = jnp.broadcast_to(decay_bwd.astype(F32)[:, None, None], (HEADS, 1, LANES))
    prompt = dict(row0=0, n_seq=N_PROMPT_SEQ, seq_len=T_PROMPT, emit_state=True)
    sample = dict(row0=PROMPT_ROWS, n_seq=N_SAMPLE_SEQ, seq_len=T_SAMPLE, emit_state=False)
    zp, sf, _ = _scan(pf, proj, gn, None, None, reverse=False, out_dtype=BF16, name="ret_scan_prompt_fwd", **prompt)
    yp, sb, _ = _scan(pb, proj, gn, None, zp, reverse=True, out_dtype=BF16, name="ret_scan_prompt_bwd", **prompt)
    half = len(casts) // 2
    zs, _, cast_f = _scan(pf, proj, gn, s0_fwd, None, reverse=False, out_dtype=BF16, casts=casts[:half],
                          name="ret_scan_sample_fwd", **sample)
    ys, _, cast_b = _scan(pb, proj, gn, s0_bwd, zs, reverse=True, out_dtype=BF16, casts=casts[half:],
                          name="ret_scan_sample_bwd", **sample)
    x = _out_proj(yp, ys, w_out.astype(BF16), jnp.zeros((D,), F32), (xp, xs), mod, 2, "ret_out_proj")
    return x, sf, sb, cast_f + cast_b


def _seq_tile_flags(i):
    n_prompt_tiles = PROMPT_ROWS // TS
    tiles_per_seq = T_SAMPLE // TS
    is_sample = i >= n_prompt_tiles
    tin = (i - n_prompt_tiles) % tiles_per_seq
    if T_PROMPT != TS:
        raise NotImplementedError("prompt sequences must be exactly one row tile")
    return is_sample & (tin > 0), is_sample & (tin < tiles_per_seq - 1)


def _conv_kernel(u_ref, up_ref, un_ref, x_ref, mod_ref, wdw_ref, bdw_ref, lng_ref, lnb_ref, w2_ref, b2_ref,
                 o_ref, ext_scr, cv_scr):
    has_prev, has_next = _seq_tile_flags(pl.program_id(0))
    n_slabs = D // LANES
    for lt in range(n_slabs):
        lanes = slice(lt * LANES, (lt + 1) * LANES)
        ext_scr[lt, 0:CONV_HALO, :] = jnp.where(has_prev, up_ref[:, lanes], 0.0)
        ext_scr[lt, CONV_HALO:CONV_HALO + TS, :] = u_ref[:, lanes]
        ext_scr[lt, CONV_HALO + TS:, :] = jnp.where(has_next, un_ref[:, lanes], 0.0)

    rows = 32
    shift = CONV_HALO - CONV_PAD

    def body(lt, carry):
        for r0 in range(0, TS, rows):
            acc = jnp.broadcast_to(bdw_ref[lt], (rows, LANES))
            for kk in range(CONV_WIDTH):
                acc = acc + wdw_ref[lt, kk:kk + 1, :] * ext_scr[lt, r0 + kk + shift:r0 + kk + shift + rows, :]
            cv_scr[lt, r0:r0 + rows, :] = acc
        return carry

    lax.fori_loop(0, n_slabs, body, 0)

    cv = jnp.concatenate([cv_scr[lt] for lt in range(n_slabs)], axis=-1)
    mu = jnp.mean(cv, axis=-1, keepdims=True)
    dlt = cv - mu
    var = jnp.mean(dlt * dlt, axis=-1, keepdims=True)
    ln = (dlt * lax.rsqrt(var + GN_EPS)) * lng_ref[...] + lnb_ref[...]
    act = _silu(ln).astype(BF16)
    y = jnp.dot(act, w2_ref[...], preferred_element_type=F32) + b2_ref[...]
    o_ref[...] = x_ref[...] + _mod_slice(mod_ref, 2) * y


def _conv_layer(x, mod, g, w_pw1, b_pw1, w_dw, b_dw, ln_g, ln_b, w_pw2, b_pw2):
    u = _glu_proj(x, mod, g, w_pw1.astype(BF16), b_pw1)
    hb = TS // CONV_HALO
    last = N_ROWS // CONV_HALO - 1
    row = lambda a: a.reshape(1, D)
    return pl.pallas_call(
        _conv_kernel,
        out_shape=jax.ShapeDtypeStruct((N_ROWS, D), F32),
        grid=(N_ROWS // TS,),
        in_specs=[
            pl.BlockSpec((TS, D), lambda i: (i, 0)),
            pl.BlockSpec((CONV_HALO, D), lambda i: (jnp.maximum(i * hb - 1, 0), 0)),
            pl.BlockSpec((CONV_HALO, D), lambda i: (jnp.minimum((i + 1) * hb, last), 0)),
            pl.BlockSpec((TS, D), lambda i: (i, 0)),
            pl.BlockSpec((None, 1, 6 * D), lambda i: (i // (GROUP_ROWS // TS), 0, 0)),
            pl.BlockSpec((D // LANES, CONV_WIDTH, LANES), lambda i: (0, 0, 0)),
            pl.BlockSpec((D // LANES, 1, LANES), lambda i: (0, 0, 0)),
            pl.BlockSpec((1, D), lambda i: (0, 0)),
            pl.BlockSpec((1, D), lambda i: (0, 0)),
            pl.BlockSpec((D, D), lambda i: (0, 0)),
            pl.BlockSpec((1, D), lambda i: (0, 0)),
        ],
        out_specs=pl.BlockSpec((TS, D), lambda i: (i, 0)),
        scratch_shapes=[pltpu.VMEM((D // LANES, TS + 2 * CONV_HALO, LANES), F32), pltpu.VMEM((D // LANES, TS, LANES), F32)],
        compiler_params=_cparams(("parallel",)),
        name="conv_dw_ln_pw2",
    )(u, u, u, x, mod, w_dw.reshape(CONV_WIDTH, D // LANES, LANES).transpose(1, 0, 2), b_dw.reshape(D // LANES, 1, LANES),
      row(ln_g), row(ln_b), w_pw2.astype(BF16), row(b_pw2))


def _window_sum(e, w):
    q, span, n = e, 1, e.shape[0]
    while 2 * span < w:
        n -= SUBLANES
        q = q[0:n] + q[span:span + n]
        span *= 2
    start = POOL_HALO - w // 2
    return q[start:start + TS] + q[start + span:start + span + TS]


def _pool_kernel(x_ref, xp_ref, xn_ref, mod_ref, g_ref, w_ref, sc_ref, o_ref, ext_scr):
    i = pl.program_id(0)
    has_prev, has_next = _seq_tile_flags(i)
    shift, scale = _mod_slice(mod_ref, 0), _mod_slice(mod_ref, 1)
    g = g_ref[...]
    x = x_ref[...]
    h = _norm_mod(x, g, shift, scale)
    ext_scr[0:POOL_HALO, :] = jnp.where(has_prev, _norm_mod(xp_ref[...], g, shift, scale), 0.0)
    ext_scr[POOL_HALO:POOL_HALO + TS, :] = h
    ext_scr[POOL_HALO + TS:2 * POOL_HALO + TS, :] = jnp.where(has_next, _norm_mod(xn_ref[...], g, shift, scale), 0.0)
    ext_scr[2 * POOL_HALO + TS:, :] = jnp.zeros((POOL_PAD, D), F32)

    seq_len = jnp.where(i >= PROMPT_ROWS // TS, T_SAMPLE, T_PROMPT)
    t = (i * TS) % seq_len + lax.broadcasted_iota(jnp.int32, (TS, LANES), 0)
    outs = []
    for gi, w in enumerate(POOL_WINDOWS):
        lanes = slice(gi * GW, (gi + 1) * GW)
        tot = _window_sum(ext_scr[:, lanes], w)
        cnt = jnp.minimum(t + w // 2, seq_len) - jnp.maximum(t - w // 2, 0)
        p = tot / jnp.concatenate([cnt.astype(F32)] * (GW // LANES), axis=-1) - h[:, lanes]
        outs.append(jnp.dot(p.astype(BF16), w_ref[gi], preferred_element_type=F32))
    y = jnp.concatenate(outs, axis=-1) * sc_ref[...]
    o_ref[...] = x + _mod_slice(mod_ref, 2) * y


def _pool_layer(x, mod, g, w_grp, scale):
    hb = TS // POOL_HALO
    last = N_ROWS // POOL_HALO - 1
    return pl.pallas_call(
        _pool_kernel,
        out_shape=jax.ShapeDtypeStruct((N_ROWS, D), F32),
        grid=(N_ROWS // TS,),
        in_specs=[
            pl.BlockSpec((TS, D), lambda i: (i, 0)),
            pl.BlockSpec((POOL_HALO, D), lambda i: (jnp.maximum(i * hb - 1, 0), 0)),
            pl.BlockSpec((POOL_HALO, D), lambda i: (jnp.minimum((i + 1) * hb, last), 0)),
            pl.BlockSpec((None, 1, 6 * D), lambda i: (i // (GROUP_ROWS // TS), 0, 0)),
            pl.BlockSpec((1, D), lambda i: (0, 0)),
            pl.BlockSpec((N_CGROUPS, GW, GW), lambda i: (0, 0, 0)),
            pl.BlockSpec((1, D), lambda i: (0, 0)),
        ],
        out_specs=pl.BlockSpec((TS, D), lambda i: (i, 0)),
        scratch_shapes=[pltpu.VMEM((TS + 2 * POOL_HALO + POOL_PAD, D), F32)],
        compiler_params=_cparams(("parallel",)),
        name="pool_mixer",
    )(x, x, x, mod, g, w_grp.astype(BF16), scale.reshape(1, D))


def _chan_dft_kernel(x_ref, mod_ref, g_ref, c_ref, s_ref, a_ref, b_ref):
    h = _norm_mod(x_ref[...], g_ref[...], _mod_slice(mod_ref, 0), _mod_slice(mod_ref, 1)).astype(BF16)
    c, s = c_ref[...], s_ref[...]
    for gi in range(N_CGROUPS):
        lanes = slice(gi * GW, (gi + 1) * GW)
        a_ref[:, lanes] = jnp.dot(h[:, lanes], c, preferred_element_type=F32).astype(a_ref.dtype)
        b_ref[:, lanes] = jnp.dot(h[:, lanes], s, preferred_element_type=F32).astype(b_ref.dtype)


def _dft_tables(n):
    idx = jnp.arange(n, dtype=jnp.int32)
    ang = (2.0 * np.pi / n) * ((idx[:, None] * idx[None, :]) % n).astype(F32)
    return jnp.cos(ang), jnp.sin(ang)


def _chan_dft(x, mod, g):
    c, s = _dft_tables(GW)
    return pl.pallas_call(
        _chan_dft_kernel,
        out_shape=[jax.ShapeDtypeStruct((N_ROWS, D), BF16)] * 2,
        grid=(N_ROWS // TM,),
        in_specs=[
            pl.BlockSpec((TM, D), lambda i: (i, 0)),
            pl.BlockSpec((None, 1, 6 * D), lambda i: (i // TILES_PER_GROUP, 0, 0)),
            pl.BlockSpec((1, D), lambda i: (0, 0)),
            pl.BlockSpec((GW, GW), lambda i: (0, 0)),
            pl.BlockSpec((GW, GW), lambda i: (0, 0)),
        ],
        out_specs=[pl.BlockSpec((TM, D), lambda i: (i, 0))] * 2,
        compiler_params=_cparams(("parallel",)),
        name="fnet_chan_dft",
    )(x, mod, g, c.astype(BF16), (-s).astype(BF16))


def _time_dft_kernel(c_ref, sn_ref, a_ref, b_ref, o_ref, acc_scr, *, n_k, scale):
    k = pl.program_id(2)

    @pl.when(k == 0)
    def _():
        acc_scr[...] = jnp.zeros_like(acc_scr)

    acc_scr[...] += (jnp.dot(c_ref[...], a_ref[...], preferred_element_type=F32)
                     + jnp.dot(sn_ref[...], b_ref[...], preferred_element_type=F32))

    @pl.when(k == n_k - 1)
    def _():
        o_ref[...] = (acc_scr[...] * scale).astype(o_ref.dtype)


def _time_dft(cos_t, sin_t, a, b, *, row0, n_seq, seq_len, name):
    tm = min(seq_len, 1024)
    tk = min(seq_len, 512)
    n_i, n_k = seq_len // tm, seq_len // tk
    kb0 = row0 // tk
    scale = 1.0 / math.sqrt(seq_len * GW)
    return pl.pallas_call(
        functools.partial(_time_dft_kernel, n_k=n_k, scale=scale),
        out_shape=jax.ShapeDtypeStruct((n_seq * seq_len, D), BF16),
        grid=(n_seq, n_i, n_k),
        in_specs=[
            pl.BlockSpec((tm, tk), lambda s, i, k: (i, k)),
            pl.BlockSpec((tm, tk), lambda s, i, k: (i, k)),
            pl.BlockSpec((tk, D), lambda s, i, k: (kb0 + s * n_k + k, 0)),
            pl.BlockSpec((tk, D), lambda s, i, k: (kb0 + s * n_k + k, 0)),
        ],
        out_specs=pl.BlockSpec((tm, D), lambda s, i, k: (s * n_i + i, 0)),
        scratch_shapes=[pltpu.VMEM((tm, D), F32)],
        compiler_params=_cparams(("parallel", "parallel", "arbitrary")),
        name=name,
    )(cos_t, sin_t, a, b)


def _cmul_const(xr, xi, wr, wi):
    def scaled(v, s):
        if s == 0.0:
            return None
        return v if s == 1.0 else (-v if s == -1.0 else v * s)

    def add(p, q):
        if p is None:
            return q
        return p if q is None else p + q

    return add(scaled(xr, wr), scaled(xi, -wi)), add(scaled(xi, wr), scaled(xr, wi))


def _fft_slabs(xr, xi):
    n = len(xr)
    if n == 1:
        return xr, xi
    er, ei = _fft_slabs(xr[0::2], xi[0::2])
    dr, di = _fft_slabs(xr[1::2], xi[1::2])
    out_r, out_i = [None] * n, [None] * n
    for k in range(n // 2):
        wr = float(round(math.cos(2.0 * math.pi * k / n), 15))
        wi = float(round(-math.sin(2.0 * math.pi * k / n), 15))
        tr, ti = _cmul_const(dr[k], di[k], wr, wi)
        out_r[k], out_i[k] = er[k] + tr, ei[k] + ti
        out_r[k + n // 2], out_i[k + n // 2] = er[k] - tr, ei[k] - ti
    return out_r, out_i


FFT_N1 = 8
FFT_N2 = T_SAMPLE // FFT_N1
FFT_LANES = 256
FFT_ROWS = 16


def _time_fft_kernel(a_ref, b_ref, twc_ref, tws_ref, m_ref, o_ref, z_scr, o_scr, *, scale):
    n_slabs = FFT_LANES // LANES
    for lt in range(n_slabs):
        lanes = slice(lt * LANES, (lt + 1) * LANES)

        def body(j, carry, lanes=lanes):
            r0 = pl.multiple_of(j * FFT_ROWS, FFT_ROWS)
            xr = [a_ref[pl.ds(s * FFT_N2 + r0, FFT_ROWS), lanes].astype(F32) for s in range(FFT_N1)]
            xi = [b_ref[pl.ds(s * FFT_N2 + r0, FFT_ROWS), lanes].astype(F32) for s in range(FFT_N1)]
            yr, yi = _fft_slabs(xr, xi)
            for c in range(FFT_N1):
                if c == 0:
                    zr, zi = yr[c], yi[c]
                else:
                    tc = twc_ref[c, pl.ds(r0, FFT_ROWS), :]
                    ts = tws_ref[c, pl.ds(r0, FFT_ROWS), :]
                    zr = yr[c] * tc + yi[c] * ts
                    zi = yi[c] * tc - yr[c] * ts
                z_scr[c, pl.ds(r0, FFT_ROWS), lanes] = zr.astype(BF16)
                z_scr[c, pl.ds(FFT_N2 + r0, FFT_ROWS), lanes] = zi.astype(BF16)
            return carry

        lax.fori_loop(0, FFT_N2 // FFT_ROWS, body, 0)

    m = m_ref[...]
    for c in range(FFT_N1):
        r = jnp.dot(m, z_scr[c], preferred_element_type=F32) * scale
        for lt in range(n_slabs):
            o_scr[lt, pl.ds(c, FFT_N2, stride=FFT_N1), :] = r[:, lt * LANES:(lt + 1) * LANES]
    for lt in range(n_slabs):
        o_ref[:, lt * LANES:(lt + 1) * LANES] = o_scr[lt].astype(o_ref.dtype)


def _time_fft_sample(a, b):
    bidx = jnp.arange(FFT_N2, dtype=jnp.int32)
    cidx = jnp.arange(FFT_N1, dtype=jnp.int32)
    ang_tw = (2.0 * np.pi / T_SAMPLE) * (cidx[:, None] * bidx[None, :]).astype(F32)
    twc = jnp.broadcast_to(jnp.cos(ang_tw)[:, :, None], (FFT_N1, FFT_N2, LANES))
    tws = jnp.broadcast_to(jnp.sin(ang_tw)[:, :, None], (FFT_N1, FFT_N2, LANES))
    c2, s2 = _dft_tables(FFT_N2)
    m = jnp.concatenate([c2, s2], axis=1).astype(BF16)
    rb0 = PROMPT_ROWS // T_SAMPLE
    return pl.pallas_call(
        functools.partial(_time_fft_kernel, scale=1.0 / math.sqrt(T_SAMPLE * GW)),
        out_shape=jax.ShapeDtypeStruct((N_SAMPLE_SEQ * T_SAMPLE, D), BF16),
        grid=(N_SAMPLE_SEQ, D // FFT_LANES),
        in_specs=[
            pl.BlockSpec((T_SAMPLE, FFT_LANES), lambda s, l: (rb0 + s, l)),
            pl.BlockSpec((T_SAMPLE, FFT_LANES), lambda s, l: (rb0 + s, l)),
            pl.BlockSpec((FFT_N1, FFT_N2, LANES), lambda s, l: (0, 0, 0)),
            pl.BlockSpec((FFT_N1, FFT_N2, LANES), lambda s, l: (0, 0, 0)),
            pl.BlockSpec((FFT_N2, 2 * FFT_N2), lambda s, l: (0, 0)),
        ],
        out_specs=pl.BlockSpec((T_SAMPLE, FFT_LANES), lambda s, l: (s, l)),
        scratch_shapes=[pltpu.VMEM((FFT_N1, 2 * FFT_N2, FFT_LANES), BF16),
                        pltpu.VMEM((FFT_LANES // LANES, T_SAMPLE, LANES), F32)],
        compiler_params=_cparams(("parallel", "parallel")),
        name="fnet_time_fft_sample",
    )(a, b, twc, tws, m)


def _fourier_layer(x, mod, g, w, b):
    a, bn = _chan_dft(x, mod, g)
    cp, sp = _dft_tables(T_PROMPT)
    fp = _time_dft(cp.astype(BF16), sp.astype(BF16), a, bn,
                   row0=0, n_seq=N_PROMPT_SEQ, seq_len=T_PROMPT, name="fnet_time_dft_prompt")
    fs = _time_fft_sample(a, bn)
    return _out_proj(fp, fs, w.astype(BF16), b, x, mod, 2, "fnet_out_proj")


def kernel(x_prompt, x_sample, state_ret_fwd, state_ret_bwd, c, c_ctx, w_mod, b_mod, norm_mix_g, norm_mlp_g, mlp_w1, mlp_w2, ret_w_in, ret_gn_g, ret_w_out, ret_decay_fwd, ret_decay_bwd, conv_w_pw1, conv_b_pw1, conv_w_dw, conv_b_dw, conv_ln_g, conv_ln_b, conv_w_pw2, conv_b_pw2, pool_w, pool_scale, fnet_w, fnet_b, final_norm_g):
    depth = w_mod.shape[0]
    x = (x_prompt.reshape(PROMPT_ROWS, D), x_sample.reshape(N_SAMPLE_SEQ * T_SAMPLE, D))
    cond = jnp.concatenate([c_ctx[None, :], c, jnp.zeros((MOD_ROWS - N_GROUPS, D), F32)], axis=0)
    mod_all = _adaln_all(cond, w_mod, b_mod).reshape(depth, MOD_ROWS, 1, 6 * D)
    final_g = final_norm_g.reshape(1, D)
    w1_all = w2_all = None
    new_f, new_b = [], []
    y_prompt = y_sample = None
    for i in range(depth):
        kind, j = i % 4, i // 4
        mod = mod_all[i]
        g_mix = norm_mix_g[i].reshape(1, D)
        if kind != 0 and isinstance(x, tuple):
            x = jnp.concatenate(x, axis=0)
        if kind == 0:
            xp, xs = x if isinstance(x, tuple) else (x[:PROMPT_ROWS], x[PROMPT_ROWS:])
            casts = [] if w1_all is not None else [mlp_w1.reshape(depth * D, D_FF), mlp_w2.reshape(depth * D_FF, D)]
            x, sf, sb, cast_out = _retention_layer(xp, xs, mod, g_mix, ret_w_in[j], ret_gn_g[j], ret_w_out[j],
                                                   ret_decay_fwd[j], ret_decay_bwd[j],
                                                   (state_ret_fwd, j), (state_ret_bwd, j), casts)
            if cast_out:
                w1_all, w2_all = cast_out[0].reshape(mlp_w1.shape), cast_out[1].reshape(mlp_w2.shape)
            new_f.append(sf)
            new_b.append(sb)
        elif kind == 1:
            x = _conv_layer(x, mod, g_mix, conv_w_pw1[j], conv_b_pw1[j], conv_w_dw[j], conv_b_dw[j],
                            conv_ln_g[j], conv_ln_b[j], conv_w_pw2[j], conv_b_pw2[j])
        elif kind == 2:
            x = _pool_layer(x, mod, g_mix, pool_w[j], pool_scale[j])
        else:
            x = _fourier_layer(x, mod, g_mix, fnet_w[j], fnet_b[j])
        g_mlp = norm_mlp_g[i].reshape(1, D)
        if w1_all is None:
            w1_all, w2_all = mlp_w1.astype(BF16), mlp_w2.astype(BF16)
        if i == depth - 1:
            npt = PROMPT_ROWS // TM
            y_prompt = _mlp(x, mod, g_mlp, w1_all, w2_all, i, final_g, 0, npt, True, "mlp_final_prompt")
            y_sample = _mlp(x, mod, g_mlp, w1_all, w2_all, i, final_g, npt, N_ROWS // TM - npt, True,
                            "mlp_final_sample")
        else:
            x = _mlp(x, mod, g_mlp, w1_all, w2_all, i, final_g, 0, N_ROWS // TM, False, "mlp")
    return (y_prompt.reshape(N_PROMPT_SEQ, T_PROMPT, D),
            y_sample.reshape(N_SAMPLE_SEQ, T_SAMPLE, D),
            jnp.stack(new_f, axis=1),
            jnp.stack(new_b, axis=1))
```

```python
import functools
import math

import numpy as np
import jax
import jax.numpy as jnp
from jax import lax
from jax.experimental import pallas as pl
from jax.experimental.pallas import tpu as pltpu

F32 = jnp.float32
BF16 = jnp.bfloat16

D = 1024
D_FF = 4 * D
N_PROMPT_SEQ = 16
T_PROMPT = 256
N_SAMPLE_SEQ = 4
T_SAMPLE = 4096
GROUP_ROWS = 4096
N_GROUPS = 1 + N_SAMPLE_SEQ
N_ROWS = N_GROUPS * GROUP_ROWS
PROMPT_ROWS = N_PROMPT_SEQ * T_PROMPT
MOD_ROWS = 16
GRID_W = 64
HEADS = 4
DK = 256
DV = 512
HK = HEADS * DK
HV = HEADS * DV
RET_CHUNK = 256
ROPE_BASE = 10000.0
CONV_WIDTH = 31
CONV_PAD = CONV_WIDTH // 2
CONV_HALO = 16
POOL_WINDOWS = (2, 4, 8, 16)
POOL_HALO = 8
POOL_PAD = 16
GW = 256
N_CGROUPS = D // GW
NORM_EPS = 1e-6
GN_EPS = 1e-5

TM = 1024
TILES_PER_GROUP = GROUP_ROWS // TM
TS = 256
LANES = 128
SUBLANES = 8
VMEM_LIMIT = 56 * 1024 * 1024


def _cparams(sem):
    return pltpu.CompilerParams(dimension_semantics=sem, vmem_limit_bytes=VMEM_LIMIT)


def _norm_mod(x, g, shift, scale):
    ms = jnp.mean(x * x, axis=-1, keepdims=True)
    y = (x * lax.rsqrt(ms + NORM_EPS)) * g
    return y * (1.0 + scale) + shift


def _mod_slice(mod_ref, idx):
    return mod_ref[:, idx * D:(idx + 1) * D]


def _silu(x):
    return x * jax.nn.sigmoid(x)


def _mod_kernel(cond_ref, w_ref, b_ref, o_ref):
    a = _silu(cond_ref[...]).astype(BF16)
    o_ref[...] = jnp.dot(a, w_ref[...].astype(BF16), preferred_element_type=F32) + b_ref[...]


def _adaln_all(cond, w_mod, b_mod):
    depth = w_mod.shape[0]
    tn = 1536
    return pl.pallas_call(
        _mod_kernel,
        out_shape=jax.ShapeDtypeStruct((depth, MOD_ROWS, 6 * D), F32),
        grid=(depth, 6 * D // tn),
        in_specs=[
            pl.BlockSpec((MOD_ROWS, D), lambda l, j: (0, 0)),
            pl.BlockSpec((None, D, tn), lambda l, j: (l, 0, j)),
            pl.BlockSpec((None, 1, tn), lambda l, j: (l, 0, j)),
        ],
        out_specs=pl.BlockSpec((None, MOD_ROWS, tn), lambda l, j: (l, 0, j)),
        compiler_params=_cparams(("parallel", "parallel")),
        name="adaln_params",
    )(cond, w_mod, b_mod.reshape(depth, 1, 6 * D))


def _resident_spec(shape):
    return pl.BlockSpec(shape, lambda *_: (0,) * len(shape), pipeline_mode=pl.Buffered(1))


def _glu_kernel(x_ref, mod_ref, g_ref, w_ref, b_ref, o_ref):
    h = _norm_mod(x_ref[...], g_ref[...], _mod_slice(mod_ref, 0), _mod_slice(mod_ref, 1)).astype(BF16)
    a = jnp.dot(h, w_ref[:, :D], preferred_element_type=F32) + b_ref[:, :D]
    gt = jnp.dot(h, w_ref[:, D:], preferred_element_type=F32) + b_ref[:, D:]
    o_ref[...] = a * jax.nn.sigmoid(gt)


def _glu_proj(x, mod, g, w, b):
    return pl.pallas_call(
        _glu_kernel,
        out_shape=jax.ShapeDtypeStruct((N_ROWS, D), F32),
        grid=(N_ROWS // TM,),
        in_specs=[
            pl.BlockSpec((TM, D), lambda i: (i, 0)),
            pl.BlockSpec((None, 1, 6 * D), lambda i: (i // TILES_PER_GROUP, 0, 0)),
            pl.BlockSpec((1, D), lambda i: (0, 0)),
            _resident_spec((D, 2 * D)),
            pl.BlockSpec((1, 2 * D), lambda i: (0, 0)),
        ],
        out_specs=pl.BlockSpec((TM, D), lambda i: (i, 0)),
        compiler_params=_cparams(("parallel",)),
        name="conv_pw1_glu",
    )(x, mod, g, w, b.reshape(1, 2 * D))


def _out_kernel(yp_ref, ys_ref, w_ref, b_ref, xp_ref, *rest, gate_idx, n_prompt_tiles):
    xs_ref = rest[0] if len(rest) == 3 else xp_ref
    mod_ref, o_ref = rest[-2:]
    i = pl.program_id(0)
    w = w_ref[...]

    def finish(y_ref, x_ref):
        acc = jnp.dot(y_ref[...], w, preferred_element_type=F32) + b_ref[...]
        o_ref[...] = x_ref[...] + _mod_slice(mod_ref, gate_idx) * acc

    @pl.when(i < n_prompt_tiles)
    def _():
        finish(yp_ref, xp_ref)

    @pl.when(i >= n_prompt_tiles)
    def _():
        finish(ys_ref, xs_ref)


def _out_proj(y_prompt, y_sample, w, b, x, mod, gate_idx, name):
    k = w.shape[0]
    npt = PROMPT_ROWS // TM
    prompt_map = lambda i: (jnp.minimum(i, npt - 1), 0)
    sample_map = lambda i: (jnp.maximum(i - npt, 0), 0)
    if isinstance(x, tuple):
        x_args = list(x)
        x_specs = [pl.BlockSpec((TM, D), prompt_map), pl.BlockSpec((TM, D), sample_map)]
    else:
        x_args = [x]
        x_specs = [pl.BlockSpec((TM, D), lambda i: (i, 0))]
    return pl.pallas_call(
        functools.partial(_out_kernel, gate_idx=gate_idx, n_prompt_tiles=npt),
        out_shape=jax.ShapeDtypeStruct((N_ROWS, D), F32),
        grid=(N_ROWS // TM,),
        in_specs=[
            pl.BlockSpec((TM, k), prompt_map),
            pl.BlockSpec((TM, k), sample_map),
            pl.BlockSpec((k, D), lambda i: (0, 0)),
            pl.BlockSpec((1, D), lambda i: (0, 0)),
        ] + x_specs + [
            pl.BlockSpec((None, 1, 6 * D), lambda i: (i // TILES_PER_GROUP, 0, 0)),
        ],
        out_specs=pl.BlockSpec((TM, D), lambda i: (i, 0)),
        compiler_params=_cparams(("parallel",)),
        name=name,
    )(y_prompt, y_sample, w, b.reshape(1, D), *x_args, mod)


def _mlp_kernel(x_ref, mod_ref, g_ref, w1_ref, w2_ref, fg_ref, *rest, tf, final_norm):
    o_ref = rest[-1]
    x = x_ref[...]
    if len(rest) == 4:
        y_ref, wo_ref, bo_ref = rest[:3]
        mixed = jnp.dot(y_ref[...], wo_ref[...], preferred_element_type=F32) + bo_ref[...]
        x = x + _mod_slice(mod_ref, 2) * mixed
    h = _norm_mod(x, g_ref[...], _mod_slice(mod_ref, 3), _mod_slice(mod_ref, 4)).astype(BF16)
    acc = None
    for f0 in range(0, D_FF, tf):
        hid = jnp.maximum(jnp.dot(h, w1_ref[:, f0:f0 + tf], preferred_element_type=F32), 0.0)
        part = jnp.dot((hid * hid).astype(BF16), w2_ref[f0:f0 + tf, :], preferred_element_type=F32)
        acc = part if acc is None else acc + part
    y = x + _mod_slice(mod_ref, 5) * acc
    if final_norm:
        ms = jnp.mean(y * y, axis=-1, keepdims=True)
        y = (y * lax.rsqrt(ms + NORM_EPS)) * fg_ref[...]
    o_ref[...] = y


def _mlp(x, mod, g, w1, w2, layer, final_g, row_tile0, n_tiles, final_norm, name, mix=None):
    layer_spec = lambda r, c: pl.BlockSpec((None, r, c), lambda i: (layer, 0, 0), pipeline_mode=pl.Buffered(1))
    mix_args, mix_specs = [], []
    if mix is not None:
        y, w_o, b_o = mix
        mix_args = [y, w_o, b_o.reshape(1, D)]
        mix_specs = [pl.BlockSpec((TM, y.shape[1]), lambda i: (i, 0)), _resident_spec(w_o.shape),
                     pl.BlockSpec((1, D), lambda i: (0, 0))]
    return pl.pallas_call(
        functools.partial(_mlp_kernel, tf=1024, final_norm=final_norm),
        out_shape=jax.ShapeDtypeStruct((n_tiles * TM, D), F32),
        grid=(n_tiles,),
        in_specs=[
            pl.BlockSpec((TM, D), lambda i: (i + row_tile0, 0)),
            pl.BlockSpec((None, 1, 6 * D), lambda i: ((i + row_tile0) // TILES_PER_GROUP, 0, 0)),
            pl.BlockSpec((1, D), lambda i: (0, 0)),
            layer_spec(D, D_FF),
            layer_spec(D_FF, D),
            pl.BlockSpec((1, D), lambda i: (0, 0)),
        ] + mix_specs,
        out_specs=pl.BlockSpec((TM, D), lambda i: (i, 0)),
        compiler_params=_cparams(("parallel",)),
        name=name,
    )(x, mod, g, w1, w2, final_g, *mix_args)


def _scan_kernel(*refs, chunk, n_chunks, reverse, has_s0, has_zin, emit_state, n_cast):
    refs = list(refs)
    p_ref, q_ref, k_ref, v_ref, g_ref, gn_ref = refs[:6]
    pos = 6
    s0_ref = zin_ref = sout_ref = None
    if has_s0:
        s0_ref = refs[pos]; pos += 1
    if has_zin:
        zin_ref = refs[pos]; pos += 1
    cast_in = refs[pos:pos + n_cast]; pos += n_cast
    z_ref = refs[pos]; pos += 1
    if emit_state:
        sout_ref = refs[pos]; pos += 1
    cast_out = refs[pos:pos + n_cast]; pos += n_cast
    s_scr, d_scr, xi_scr, zeta_scr = refs[pos:pos + 4]
    for src, dst in zip(cast_in, cast_out):
        dst[...] = src[...].astype(dst.dtype)

    c = pl.program_id(1)
    log_g = [jnp.log1p(-jnp.exp2(-p_ref[h]))[:, :1] for h in range(HEADS)]

    @pl.when((c == 0) & (pl.program_id(0) == 0))
    def _():
        ri = lax.broadcasted_iota(jnp.int32, (chunk, chunk), 0)
        ci = lax.broadcasted_iota(jnp.int32, (chunk, chunk), 1)
        rel = (ci - ri) if reverse else (ri - ci)
        relf = jnp.maximum(rel, 0).astype(F32)
        t = lax.broadcasted_iota(jnp.int32, (chunk, LANES), 0)
        step = ((chunk - 1) - t if reverse else t).astype(F32)
        for h in range(HEADS):
            d_scr[h] = jnp.where(rel >= 0, jnp.exp(log_g[h] * relf), 0.0)
            xi_scr[h] = jnp.exp(log_g[h] * (step + 1.0))
            zeta_scr[h] = jnp.exp(log_g[h] * ((chunk - 1.0) - step))

    @pl.when(c == 0)
    def _():
        if has_s0:
            s_scr[...] = s0_ref[...]
        else:
            s_scr[...] = jnp.zeros_like(s_scr)

    for h in range(HEADS):
        kcols = slice(h * DK, (h + 1) * DK)
        vcols = slice(h * DV, (h + 1) * DV)
        q = q_ref[:, kcols]
        k = k_ref[:, kcols]
        v = v_ref[:, vcols]
        s = s_scr[h]
        scores = lax.dot_general(q, k, (((1,), (1,)), ((), ())), preferred_element_type=F32)
        inner = jnp.dot((scores * d_scr[h]).astype(BF16), v, preferred_element_type=F32)
        cross = jnp.dot(q, s.astype(BF16), preferred_element_type=F32)
        xi = jnp.concatenate([xi_scr[h]] * (DV // LANES), axis=-1)
        o = inner + cross * xi
        zeta = jnp.concatenate([zeta_scr[h]] * (DK // LANES), axis=-1)
        kz = (k.astype(F32) * zeta).astype(BF16)
        upd = lax.dot_general(kz, v, (((0,), (0,)), ((), ())), preferred_element_type=F32)
        s_scr[h] = s * jnp.exp(log_g[h] * float(chunk)) + upd

        mu = jnp.mean(o, axis=-1, keepdims=True)
        dlt = o - mu
        var = jnp.mean(dlt * dlt, axis=-1, keepdims=True)
        z = _silu(g_ref[:, vcols].astype(F32)) * ((dlt * lax.rsqrt(var + GN_EPS)) * gn_ref[:, vcols])
        if has_zin:
            z = zin_ref[:, vcols].astype(F32) + z
        z_ref[:, vcols] = z.astype(z_ref.dtype)

    if emit_state:
        @pl.when(c == n_chunks - 1)
        def _():
            sout_ref[...] = s_scr[...]


def _scan(decay_p, proj, gn_g, s0, zin, *, row0, n_seq, seq_len, reverse, emit_state, out_dtype, name, casts=()):
    chunk = RET_CHUNK
    nc = seq_len // chunk
    rb0 = row0 // chunk

    def rb(b, c):
        cc = (nc - 1 - c) if reverse else c
        return b * nc + cc

    in_specs = [
        pl.BlockSpec((HEADS, 1, LANES), lambda b, c: (0, 0, 0)),
        pl.BlockSpec((chunk, HK), lambda b, c: (rb0 + rb(b, c), 0)),
        pl.BlockSpec((chunk, HK), lambda b, c: (rb0 + rb(b, c), 1)),
        pl.BlockSpec((chunk, HV), lambda b, c: (rb0 + rb(b, c), 1)),
        pl.BlockSpec((chunk, HV), lambda b, c: (rb0 + rb(b, c), 3 if reverse else 2)),
        pl.BlockSpec((1, HV), lambda b, c: (0, 0)),
    ]
    args = [decay_p, proj, proj, proj, proj, gn_g]
    if s0 is not None:
        s0_arr, s0_layer = s0
        in_specs.append(pl.BlockSpec((None, None, HEADS, DK, DV), lambda b, c: (b, s0_layer, 0, 0, 0)))
        args.append(s0_arr)
    if zin is not None:
        in_specs.append(pl.BlockSpec((chunk, HV), lambda b, c: (rb(b, c), 0)))
        args.append(zin)
    out_shape = [jax.ShapeDtypeStruct((n_seq * seq_len, HV), out_dtype)]
    out_specs = [pl.BlockSpec((chunk, HV), lambda b, c: (rb(b, c), 0))]
    if emit_state:
        out_shape.append(jax.ShapeDtypeStruct((n_seq, HEADS, DK, DV), F32))
        out_specs.append(pl.BlockSpec((None, HEADS, DK, DV), lambda b, c: (b, 0, 0, 0)))
    for arr in casts:
        rows, cols = arr.shape
        blk = pl.BlockSpec((rows // (n_seq * nc), cols), lambda b, c: (b * nc + c, 0))
        in_specs.append(blk)
        args.append(arr)
        out_shape.append(jax.ShapeDtypeStruct(arr.shape, BF16))
        out_specs.append(blk)
    res = pl.pallas_call(
        functools.partial(_scan_kernel, chunk=chunk, n_chunks=nc, reverse=reverse, has_s0=s0 is not None,
                          has_zin=zin is not None, emit_state=emit_state, n_cast=len(casts)),
        out_shape=out_shape,
        grid=(n_seq, nc),
        in_specs=in_specs,
        out_specs=out_specs,
        scratch_shapes=[pltpu.VMEM((HEADS, DK, DV), F32), pltpu.VMEM((HEADS, chunk, chunk), F32),
                        pltpu.VMEM((HEADS, chunk, LANES), F32), pltpu.VMEM((HEADS, chunk, LANES), F32)],
        compiler_params=_cparams(("arbitrary", "arbitrary")),
        name=name,
    )(*args)
    n_main = 2 if emit_state else 1
    return res[0], (res[1] if emit_state else None), list(res[n_main:])


def _rope_tables():
    n = DK // 4
    inv = ROPE_BASE ** (-jnp.arange(n, dtype=F32) / n)
    t = jnp.arange(T_SAMPLE, dtype=jnp.int32)
    row = (t // GRID_W).astype(F32)
    col = (t % GRID_W).astype(F32)
    ang = jnp.concatenate([row[:, None] * inv[None, :], col[:, None] * inv[None, :]], axis=-1)
    cos = jnp.concatenate([jnp.ones((PROMPT_ROWS, DK // 2), F32)] + [jnp.cos(ang)] * N_SAMPLE_SEQ, axis=0)
    sin = jnp.concatenate([jnp.zeros((PROMPT_ROWS, DK // 2), F32)] + [jnp.sin(ang)] * N_SAMPLE_SEQ, axis=0)
    return cos, sin


def _ret_proj_kernel(xp_ref, xs_ref, mod_ref, g_ref, w_ref, cos_ref, sin_ref, o_ref, *, n_prompt_tiles):
    x = jnp.where(pl.program_id(0) < n_prompt_tiles, xp_ref[...], xs_ref[...])
    h = _norm_mod(x, g_ref[...], _mod_slice(mod_ref, 0), _mod_slice(mod_ref, 1)).astype(BF16)
    half = DK // 2
    cos, sin = cos_ref[...], sin_ref[...]
    for j in range(w_ref.shape[1] // HK):
        acc = jnp.dot(h, w_ref[:, j * HK:(j + 1) * HK], preferred_element_type=F32)
        if j < 2:
            kscale = DK ** -0.5 if j == 1 else 1.0
            for hd in range(HEADS):
                c0 = j * HK + hd * DK
                x1 = acc[:, hd * DK:hd * DK + half]
                x2 = acc[:, hd * DK + half:(hd + 1) * DK]
                o_ref[:, c0:c0 + half] = ((x1 * cos - x2 * sin) * kscale).astype(o_ref.dtype)
                o_ref[:, c0 + half:c0 + DK] = ((x2 * cos + x1 * sin) * kscale).astype(o_ref.dtype)
        else:
            o_ref[:, j * HK:(j + 1) * HK] = acc.astype(o_ref.dtype)


def _ret_proj(xp, xs, mod, g, w_in, cos, sin):
    tm = TM // 2
    npt = PROMPT_ROWS // tm
    half = DK // 2
    n_cols = w_in.shape[1]
    return pl.pallas_call(
        functools.partial(_ret_proj_kernel, n_prompt_tiles=npt),
        out_shape=jax.ShapeDtypeStruct((N_ROWS, n_cols), BF16),
        grid=(N_ROWS // tm,),
        in_specs=[
            pl.BlockSpec((tm, D), lambda i: (jnp.minimum(i, npt - 1), 0)),
            pl.BlockSpec((tm, D), lambda i: (jnp.maximum(i - npt, 0), 0)),
            pl.BlockSpec((None, 1, 6 * D), lambda i: (i // (GROUP_ROWS // tm), 0, 0)),
            pl.BlockSpec((1, D), lambda i: (0, 0)),
            _resident_spec((D, n_cols)),
            pl.BlockSpec((tm, half), lambda i: (i, 0)),
            pl.BlockSpec((tm, half), lambda i: (i, 0)),
        ],
        out_specs=pl.BlockSpec((tm, n_cols), lambda i: (i, 0)),
        compiler_params=_cparams(("parallel",)),
        name="ret_in_proj",
    )(xp, xs, mod, g, w_in, cos, sin)


def _retention_layer(xp, xs, mod, g, w_in, gn_g, w_out, decay_fwd, decay_bwd, s0_fwd, s0_bwd, casts):
    cos, sin = _rope_tables()
    proj = _ret_proj(xp, xs, mod, g, w_in.astype(BF16), cos, sin)
    gn = gn_g.reshape(1, HV)
    pf = jnp.broadcast_to(decay_fwd.astype(F32)[:, None, None], (HEADS, 1, LANES))
    pb = jnp.broadcast_to(decay_bwd.astype(F32)[:, None, None], (HEADS, 1, LANES))
    prompt = dict(row0=0, n_seq=N_PROMPT_SEQ, seq_len=T_PROMPT, emit_state=True)
    sample = dict(row0=PROMPT_ROWS, n_seq=N_SAMPLE_SEQ, seq_len=T_SAMPLE, emit_state=False)
    zp, sf, _ = _scan(pf, proj, gn, None, None, reverse=False, out_dtype=BF16, name="ret_scan_prompt_fwd", **prompt)
    yp, sb, _ = _scan(pb, proj, gn, None, zp, reverse=True, out_dtype=BF16, name="ret_scan_prompt_bwd", **prompt)
    half = len(casts) // 2
    zs, _, cast_f = _scan(pf, proj, gn, s0_fwd, None, reverse=False, out_dtype=BF16, casts=casts[:half],
                          name="ret_scan_sample_fwd", **sample)
    ys, _, cast_b = _scan(pb, proj, gn, s0_bwd, zs, reverse=True, out_dtype=BF16, casts=casts[half:],
                          name="ret_scan_sample_bwd", **sample)
    x = _out_proj(yp, ys, w_out.astype(BF16), jnp.zeros((D,), F32), (xp, xs), mod, 2, "ret_out_proj")
    return x, sf, sb, cast_f + cast_b


def _seq_tile_flags(i):
    n_prompt_tiles = PROMPT_ROWS // TS
    tiles_per_seq = T_SAMPLE // TS
    is_sample = i >= n_prompt_tiles
    tin = (i - n_prompt_tiles) % tiles_per_seq
    if T_PROMPT != TS:
        raise NotImplementedError("prompt sequences must be exactly one row tile")
    return is_sample & (tin > 0), is_sample & (tin < tiles_per_seq - 1)


def _conv_kernel(u_ref, up_ref, un_ref, x_ref, mod_ref, wdw_ref, bdw_ref, lng_ref, lnb_ref, w2_ref, b2_ref,
                 o_ref, ext_scr, cv_scr):
    has_prev, has_next = _seq_tile_flags(pl.program_id(0))
    n_slabs = D // LANES
    for lt in range(n_slabs):
        lanes = slice(lt * LANES, (lt + 1) * LANES)
        ext_scr[lt, 0:CONV_HALO, :] = jnp.where(has_prev, up_ref[:, lanes], 0.0)
        ext_scr[lt, CONV_HALO:CONV_HALO + TS, :] = u_ref[:, lanes]
        ext_scr[lt, CONV_HALO + TS:, :] = jnp.where(has_next, un_ref[:, lanes], 0.0)

    rows = 32
    shift = CONV_HALO - CONV_PAD

    def body(lt, carry):
        for r0 in range(0, TS, rows):
            acc = jnp.broadcast_to(bdw_ref[lt], (rows, LANES))
            for kk in range(CONV_WIDTH):
                acc = acc + wdw_ref[lt, kk:kk + 1, :] * ext_scr[lt, r0 + kk + shift:r0 + kk + shift + rows, :]
            cv_scr[lt, r0:r0 + rows, :] = acc
        return carry

    lax.fori_loop(0, n_slabs, body, 0)

    cv = jnp.concatenate([cv_scr[lt] for lt in range(n_slabs)], axis=-1)
    mu = jnp.mean(cv, axis=-1, keepdims=True)
    dlt = cv - mu
    var = jnp.mean(dlt * dlt, axis=-1, keepdims=True)
    ln = (dlt * lax.rsqrt(var + GN_EPS)) * lng_ref[...] + lnb_ref[...]
    act = _silu(ln).astype(BF16)
    y = jnp.dot(act, w2_ref[...], preferred_element_type=F32) + b2_ref[...]
    o_ref[...] = x_ref[...] + _mod_slice(mod_ref, 2) * y


def _conv_layer(x, mod, g, w_pw1, b_pw1, w_dw, b_dw, ln_g, ln_b, w_pw2, b_pw2):
    u = _glu_proj(x, mod, g, w_pw1.astype(BF16), b_pw1)
    hb = TS // CONV_HALO
    last = N_ROWS // CONV_HALO - 1
    row = lambda a: a.reshape(1, D)
    return pl.pallas_call(
        _conv_kernel,
        out_shape=jax.ShapeDtypeStruct((N_ROWS, D), F32),
        grid=(N_ROWS // TS,),
        in_specs=[
            pl.BlockSpec((TS, D), lambda i: (i, 0)),
            pl.BlockSpec((CONV_HALO, D), lambda i: (jnp.maximum(i * hb - 1, 0), 0)),
            pl.BlockSpec((CONV_HALO, D), lambda i: (jnp.minimum((i + 1) * hb, last), 0)),
            pl.BlockSpec((TS, D), lambda i: (i, 0)),
            pl.BlockSpec((None, 1, 6 * D), lambda i: (i // (GROUP_ROWS // TS), 0, 0)),
            pl.BlockSpec((D // LANES, CONV_WIDTH, LANES), lambda i: (0, 0, 0)),
            pl.BlockSpec((D // LANES, 1, LANES), lambda i: (0, 0, 0)),
            pl.BlockSpec((1, D), lambda i: (0, 0)),
            pl.BlockSpec((1, D), lambda i: (0, 0)),
            pl.BlockSpec((D, D), lambda i: (0, 0)),
            pl.BlockSpec((1, D), lambda i: (0, 0)),
        ],
        out_specs=pl.BlockSpec((TS, D), lambda i: (i, 0)),
        scratch_shapes=[pltpu.VMEM((D // LANES, TS + 2 * CONV_HALO, LANES), F32), pltpu.VMEM((D // LANES, TS, LANES), F32)],
        compiler_params=_cparams(("parallel",)),
        name="conv_dw_ln_pw2",
    )(u, u, u, x, mod, w_dw.reshape(CONV_WIDTH, D // LANES, LANES).transpose(1, 0, 2), b_dw.reshape(D // LANES, 1, LANES),
      row(ln_g), row(ln_b), w_pw2.astype(BF16), row(b_pw2))


def _window_sum(e, w):
    q, span, n = e, 1, e.shape[0]
    while 2 * span < w:
        n -= SUBLANES
        q = q[0:n] + q[span:span + n]
        span *= 2
    start = POOL_HALO - w // 2
    return q[start:start + TS] + q[start + span:start + span + TS]


def _pool_kernel(x_ref, xp_ref, xn_ref, mod_ref, g_ref, w_ref, sc_ref, o_ref, ext_scr):
    i = pl.program_id(0)
    has_prev, has_next = _seq_tile_flags(i)
    shift, scale = _mod_slice(mod_ref, 0), _mod_slice(mod_ref, 1)
    g = g_ref[...]
    x = x_ref[...]
    h = _norm_mod(x, g, shift, scale)
    ext_scr[0:POOL_HALO, :] = jnp.where(has_prev, _norm_mod(xp_ref[...], g, shift, scale), 0.0)
    ext_scr[POOL_HALO:POOL_HALO + TS, :] = h
    ext_scr[POOL_HALO + TS:2 * POOL_HALO + TS, :] = jnp.where(has_next, _norm_mod(xn_ref[...], g, shift, scale), 0.0)
    ext_scr[2 * POOL_HALO + TS:, :] = jnp.zeros((POOL_PAD, D), F32)

    seq_len = jnp.where(i >= PROMPT_ROWS // TS, T_SAMPLE, T_PROMPT)
    t = (i * TS) % seq_len + lax.broadcasted_iota(jnp.int32, (TS, LANES), 0)
    outs = []
    for gi, w in enumerate(POOL_WINDOWS):
        lanes = slice(gi * GW, (gi + 1) * GW)
        tot = _window_sum(ext_scr[:, lanes], w)
        cnt = jnp.minimum(t + w // 2, seq_len) - jnp.maximum(t - w // 2, 0)
        p = tot / jnp.concatenate([cnt.astype(F32)] * (GW // LANES), axis=-1) - h[:, lanes]
        outs.append(jnp.dot(p.astype(BF16), w_ref[gi], preferred_element_type=F32))
    y = jnp.concatenate(outs, axis=-1) * sc_ref[...]
    o_ref[...] = x + _mod_slice(mod_ref, 2) * y


def _pool_layer(x, mod, g, w_grp, scale):
    hb = TS // POOL_HALO
    last = N_ROWS // POOL_HALO - 1
    return pl.pallas_call(
        _pool_kernel,
        out_shape=jax.ShapeDtypeStruct((N_ROWS, D), F32),
        grid=(N_ROWS // TS,),
        in_specs=[
            pl.BlockSpec((TS, D), lambda i: (i, 0)),
            pl.BlockSpec((POOL_HALO, D), lambda i: (jnp.maximum(i * hb - 1, 0), 0)),
            pl.BlockSpec((POOL_HALO, D), lambda i: (jnp.minimum((i + 1) * hb, last), 0)),
            pl.BlockSpec((None, 1, 6 * D), lambda i: (i // (GROUP_ROWS // TS), 0, 0)),
            pl.BlockSpec((1, D), lambda i: (0, 0)),
            pl.BlockSpec((N_CGROUPS, GW, GW), lambda i: (0, 0, 0)),
            pl.BlockSpec((1, D), lambda i: (0, 0)),
        ],
        out_specs=pl.BlockSpec((TS, D), lambda i: (i, 0)),
        scratch_shapes=[pltpu.VMEM((TS + 2 * POOL_HALO + POOL_PAD, D), F32)],
        compiler_params=_cparams(("parallel",)),
        name="pool_mixer",
    )(x, x, x, mod, g, w_grp.astype(BF16), scale.reshape(1, D))


def _chan_dft_kernel(x_ref, mod_ref, g_ref, c_ref, s_ref, a_ref, b_ref):
    h = _norm_mod(x_ref[...], g_ref[...], _mod_slice(mod_ref, 0), _mod_slice(mod_ref, 1)).astype(BF16)
    c, s = c_ref[...], s_ref[...]
    for gi in range(N_CGROUPS):
        lanes = slice(gi * GW, (gi + 1) * GW)
        a_ref[:, lanes] = jnp.dot(h[:, lanes], c, preferred_element_type=F32).astype(a_ref.dtype)
        b_ref[:, lanes] = jnp.dot(h[:, lanes], s, preferred_element_type=F32).astype(b_ref.dtype)


def _dft_tables(n):
    idx = jnp.arange(n, dtype=jnp.int32)
    ang = (2.0 * np.pi / n) * ((idx[:, None] * idx[None, :]) % n).astype(F32)
    return jnp.cos(ang), jnp.sin(ang)


def _chan_dft(x, mod, g):
    c, s = _dft_tables(GW)
    return pl.pallas_call(
        _chan_dft_kernel,
        out_shape=[jax.ShapeDtypeStruct((N_ROWS, D), BF16)] * 2,
        grid=(N_ROWS // TM,),
        in_specs=[
            pl.BlockSpec((TM, D), lambda i: (i, 0)),
            pl.BlockSpec((None, 1, 6 * D), lambda i: (i // TILES_PER_GROUP, 0, 0)),
            pl.BlockSpec((1, D), lambda i: (0, 0)),
            pl.BlockSpec((GW, GW), lambda i: (0, 0)),
            pl.BlockSpec((GW, GW), lambda i: (0, 0)),
        ],
        out_specs=[pl.BlockSpec((TM, D), lambda i: (i, 0))] * 2,
        compiler_params=_cparams(("parallel",)),
        name="fnet_chan_dft",
    )(x, mod, g, c.astype(BF16), (-s).astype(BF16))


def _time_dft_kernel(c_ref, sn_ref, a_ref, b_ref, o_ref, acc_scr, *, n_k, scale):
    k = pl.program_id(2)

    @pl.when(k == 0)
    def _():
        acc_scr[...] = jnp.zeros_like(acc_scr)

    acc_scr[...] += (jnp.dot(c_ref[...], a_ref[...], preferred_element_type=F32)
                     + jnp.dot(sn_ref[...], b_ref[...], preferred_element_type=F32))

    @pl.when(k == n_k - 1)
    def _():
        o_ref[...] = (acc_scr[...] * scale).astype(o_ref.dtype)


def _time_dft(cos_t, sin_t, a, b, *, row0, n_seq, seq_len, name):
    tm = min(seq_len, 1024)
    tk = min(seq_len, 512)
    n_i, n_k = seq_len // tm, seq_len // tk
    kb0 = row0 // tk
    scale = 1.0 / math.sqrt(seq_len * GW)
    return pl.pallas_call(
        functools.partial(_time_dft_kernel, n_k=n_k, scale=scale),
        out_shape=jax.ShapeDtypeStruct((n_seq * seq_len, D), BF16),
        grid=(n_seq, n_i, n_k),
        in_specs=[
            pl.BlockSpec((tm, tk), lambda s, i, k: (i, k)),
            pl.BlockSpec((tm, tk), lambda s, i, k: (i, k)),
            pl.BlockSpec((tk, D), lambda s, i, k: (kb0 + s * n_k + k, 0)),
            pl.BlockSpec((tk, D), lambda s, i, k: (kb0 + s * n_k + k, 0)),
        ],
        out_specs=pl.BlockSpec((tm, D), lambda s, i, k: (s * n_i + i, 0)),
        scratch_shapes=[pltpu.VMEM((tm, D), F32)],
        compiler_params=_cparams(("parallel", "parallel", "arbitrary")),
        name=name,
    )(cos_t, sin_t, a, b)


def _cmul_const(xr, xi, wr, wi):
    def scaled(v, s):
        if s == 0.0:
            return None
        return v if s == 1.0 else (-v if s == -1.0 else v * s)

    def add(p, q):
        if p is None:
            return q
        return p if q is None else p + q

    return add(scaled(xr, wr), scaled(xi, -wi)), add(scaled(xi, wr), scaled(xr, wi))


def _fft_slabs(xr, xi):
    n = len(xr)
    if n == 1:
        return xr, xi
    er, ei = _fft_slabs(xr[0::2], xi[0::2])
    dr, di = _fft_slabs(xr[1::2], xi[1::2])
    out_r, out_i = [None] * n, [None] * n
    for k in range(n // 2):
        wr = float(round(math.cos(2.0 * math.pi * k / n), 15))
        wi = float(round(-math.sin(2.0 * math.pi * k / n), 15))
        tr, ti = _cmul_const(dr[k], di[k], wr, wi)
        out_r[k], out_i[k] = er[k] + tr, ei[k] + ti
        out_r[k + n // 2], out_i[k + n // 2] = er[k] - tr, ei[k] - ti
    return out_r, out_i


FFT_N1 = 8
FFT_N2 = T_SAMPLE // FFT_N1
FFT_LANES = 256
FFT_ROWS = 16


def _time_fft_kernel(a_ref, b_ref, twc_ref, tws_ref, m_ref, o_ref, z_scr, o_scr, *, scale):
    n_slabs = FFT_LANES // LANES
    for lt in range(n_slabs):
        lanes = slice(lt * LANES, (lt + 1) * LANES)

        def body(j, carry, lanes=lanes):
            r0 = pl.multiple_of(j * FFT_ROWS, FFT_ROWS)
            xr = [a_ref[pl.ds(s * FFT_N2 + r0, FFT_ROWS), lanes].astype(F32) for s in range(FFT_N1)]
            xi = [b_ref[pl.ds(s * FFT_N2 + r0, FFT_ROWS), lanes].astype(F32) for s in range(FFT_N1)]
            yr, yi = _fft_slabs(xr, xi)
            for c in range(FFT_N1):
                if c == 0:
                    zr, zi = yr[c], yi[c]
                else:
                    tc = twc_ref[c, pl.ds(r0, FFT_ROWS), :]
                    ts = tws_ref[c, pl.ds(r0, FFT_ROWS), :]
                    zr = yr[c] * tc + yi[c] * ts
                    zi = yi[c] * tc - yr[c] * ts
                z_scr[c, pl.ds(r0, FFT_ROWS), lanes] = zr.astype(BF16)
                z_scr[c, pl.ds(FFT_N2 + r0, FFT_ROWS), lanes] = zi.astype(BF16)
            return carry

        lax.fori_loop(0, FFT_N2 // FFT_ROWS, body, 0)

    m = m_ref[...]
    for c in range(FFT_N1):
        r = jnp.dot(m, z_scr[c], preferred_element_type=F32) * scale
        for lt in range(n_slabs):
            o_scr[lt, pl.ds(c, FFT_N2, stride=FFT_N1), :] = r[:, lt * LANES:(lt + 1) * LANES]
    for lt in range(n_slabs):
        o_ref[:, lt * LANES:(lt + 1) * LANES] = o_scr[lt].astype(o_ref.dtype)


def _time_fft_sample(a, b):
    bidx = jnp.arange(FFT_N2, dtype=jnp.int32)
    cidx = jnp.arange(FFT_N1, dtype=jnp.int32)
    ang_tw = (2.0 * np.pi / T_SAMPLE) * (cidx[:, None] * bidx[None, :]).astype(F32)
    twc = jnp.broadcast_to(jnp.cos(ang_tw)[:, :, None], (FFT_N1, FFT_N2, LANES))
    tws = jnp.broadcast_to(jnp.sin(ang_tw)[:, :, None], (FFT_N1, FFT_N2, LANES))
    c2, s2 = _dft_tables(FFT_N2)
    m = jnp.concatenate([c2, s2], axis=1).astype(BF16)
    rb0 = PROMPT_ROWS // T_SAMPLE
    return pl.pallas_call(
        functools.partial(_time_fft_kernel, scale=1.0 / math.sqrt(T_SAMPLE * GW)),
        out_shape=jax.ShapeDtypeStruct((N_SAMPLE_SEQ * T_SAMPLE, D), BF16),
        grid=(N_SAMPLE_SEQ, D // FFT_LANES),
        in_specs=[
            pl.BlockSpec((T_SAMPLE, FFT_LANES), lambda s, l: (rb0 + s, l)),
            pl.BlockSpec((T_SAMPLE, FFT_LANES), lambda s, l: (rb0 + s, l)),
            pl.BlockSpec((FFT_N1, FFT_N2, LANES), lambda s, l: (0, 0, 0)),
            pl.BlockSpec((FFT_N1, FFT_N2, LANES), lambda s, l: (0, 0, 0)),
            pl.BlockSpec((FFT_N2, 2 * FFT_N2), lambda s, l: (0, 0)),
        ],
        out_specs=pl.BlockSpec((T_SAMPLE, FFT_LANES), lambda s, l: (s, l)),
        scratch_shapes=[pltpu.VMEM((FFT_N1, 2 * FFT_N2, FFT_LANES), BF16),
                        pltpu.VMEM((FFT_LANES // LANES, T_SAMPLE, LANES), F32)],
        compiler_params=_cparams(("parallel", "parallel")),
        name="fnet_time_fft_sample",
    )(a, b, twc, tws, m)


def _fourier_features(x, mod, g):
    a, bn = _chan_dft(x, mod, g)
    cp, sp = _dft_tables(T_PROMPT)
    fp = _time_dft(cp.astype(BF16), sp.astype(BF16), a, bn,
                   row0=0, n_seq=N_PROMPT_SEQ, seq_len=T_PROMPT, name="fnet_time_dft_prompt")
    return fp, _time_fft_sample(a, bn)


def _fourier_layer(x, mod, g, w, b):
    fp, fs = _fourier_features(x, mod, g)
    return _out_proj(fp, fs, w.astype(BF16), b, x, mod, 2, "fnet_out_proj")


def kernel(x_prompt, x_sample, state_ret_fwd, state_ret_bwd, c, c_ctx, w_mod, b_mod, norm_mix_g, norm_mlp_g, mlp_w1, mlp_w2, ret_w_in, ret_gn_g, ret_w_out, ret_decay_fwd, ret_decay_bwd, conv_w_pw1, conv_b_pw1, conv_w_dw, conv_b_dw, conv_ln_g, conv_ln_b, conv_w_pw2, conv_b_pw2, pool_w, pool_scale, fnet_w, fnet_b, final_norm_g):
    depth = w_mod.shape[0]
    x = (x_prompt.reshape(PROMPT_ROWS, D), x_sample.reshape(N_SAMPLE_SEQ * T_SAMPLE, D))
    cond = jnp.concatenate([c_ctx[None, :], c, jnp.zeros((MOD_ROWS - N_GROUPS, D), F32)], axis=0)
    mod_all = _adaln_all(cond, w_mod, b_mod).reshape(depth, MOD_ROWS, 1, 6 * D)
    final_g = final_norm_g.reshape(1, D)
    w1_all = w2_all = None
    new_f, new_b = [], []
    y_prompt = y_sample = None
    for i in range(depth):
        kind, j = i % 4, i // 4
        mod = mod_all[i]
        g_mix = norm_mix_g[i].reshape(1, D)
        mix_prompt = mix_sample = None
        if kind != 0 and isinstance(x, tuple):
            x = jnp.concatenate(x, axis=0)
        if kind == 0:
            xp, xs = x if isinstance(x, tuple) else (x[:PROMPT_ROWS], x[PROMPT_ROWS:])
            casts = [] if w1_all is not None else [mlp_w1.reshape(depth * D, D_FF), mlp_w2.reshape(depth * D_FF, D)]
            x, sf, sb, cast_out = _retention_layer(xp, xs, mod, g_mix, ret_w_in[j], ret_gn_g[j], ret_w_out[j],
                                                   ret_decay_fwd[j], ret_decay_bwd[j],
                                                   (state_ret_fwd, j), (state_ret_bwd, j), casts)
            if cast_out:
                w1_all, w2_all = cast_out[0].reshape(mlp_w1.shape), cast_out[1].reshape(mlp_w2.shape)
            new_f.append(sf)
            new_b.append(sb)
        elif kind == 1:
            x = _conv_layer(x, mod, g_mix, conv_w_pw1[j], conv_b_pw1[j], conv_w_dw[j], conv_b_dw[j],
                            conv_ln_g[j], conv_ln_b[j], conv_w_pw2[j], conv_b_pw2[j])
        elif kind == 2:
            x = _pool_layer(x, mod, g_mix, pool_w[j], pool_scale[j])
        elif i == depth - 1:
            fp, fs = _fourier_features(x, mod, g_mix)
            w_o = fnet_w[j].astype(BF16)
            mix_prompt, mix_sample = (fp, w_o, fnet_b[j]), (fs, w_o, fnet_b[j])
        else:
            x = _fourier_layer(x, mod, g_mix, fnet_w[j], fnet_b[j])
        g_mlp = norm_mlp_g[i].reshape(1, D)
        if w1_all is None:
            w1_all, w2_all = mlp_w1.astype(BF16), mlp_w2.astype(BF16)
        if i == depth - 1:
            npt = PROMPT_ROWS // TM
            y_prompt = _mlp(x, mod, g_mlp, w1_all, w2_all, i, final_g, 0, npt, True, "mlp_final_prompt",
                            mix=mix_prompt)
            y_sample = _mlp(x, mod, g_mlp, w1_all, w2_all, i, final_g, npt, N_ROWS // TM - npt, True,
                            "mlp_final_sample", mix=mix_sample)
        else:
            x = _mlp(x, mod, g_mlp, w1_all, w2_all, i, final_g, 0, N_ROWS // TM, False, "mlp")
    return (y_prompt.reshape(N_PROMPT_SEQ, T_PROMPT, D),
            y_sample.reshape(N_SAMPLE_SEQ, T_SAMPLE, D),
            jnp.stack(new_f, axis=1),
            jnp.stack(new_b, axis=1))
```
